```python
import math
import jax, jax.numpy as jnp
from jax import lax
import numpy as np

D_MODEL = 2048
BATCH = 4
SEQ = 2048
DEPTH = 4
DEC_BATCH = 128
DEC_SEQ = 1
PAST_LEN = 16384
PAGE_SIZE = 128

D_RWKV = D_MODEL // 2
HEAD_DIM = 64
N_HEADS = D_RWKV // HEAD_DIM
R_DECAY = 64
R_AAA = 64
R_GATE = 128
D_CONV = D_MODEL // 2
CONV_W = 3
D_SHIFT = 3 * D_RWKV + R_DECAY + R_AAA + R_GATE
D_IN = D_SHIFT + 3 * D_CONV + 2 * D_MODEL
D_FF = int(math.ceil(8 * D_MODEL / 3 / 256) * 256)
PLE_DIM = 256
RMS_EPS = 1e-6
GN_EPS = 64e-5

kernel_name = "rwkv7_shortconv_gated_hybrid_step"


def rms_norm(x, g):
    xf = x.astype(jnp.float32)
    y = xf * lax.rsqrt(jnp.mean(xf * xf, axis=-1, keepdims=True) + RMS_EPS)
    return (y * g.astype(jnp.float32)).astype(x.dtype)


def rms_unit(x):
    xf = x.astype(jnp.float32)
    return (xf * lax.rsqrt(jnp.mean(xf * xf, axis=-1, keepdims=True) + RMS_EPS)).astype(x.dtype)


def wkv7_scan(r, decay, k, v, kk, a, S0):
    def step(S, inp):
        r_t, w_t, k_t, v_t, kk_t, a_t = inp
        sa = jnp.einsum('bhvk,bhk->bhv', S, kk_t)
        S = (S * w_t[:, :, None, :]
             - sa[..., None] * (kk_t * a_t)[:, :, None, :]
             + v_t[..., None] * k_t[:, :, None, :])
        o = jnp.einsum('bhvk,bhk->bhv', S, r_t)
        return S, o
    xs = tuple(jnp.moveaxis(t, 1, 0) for t in (r, decay, k, v, kk, a))
    S, o = lax.scan(step, S0, xs)
    return jnp.moveaxis(o, 0, 1), S


def hybrid_layer(x, p, S0, shift_prev, conv_buf,
                 g_mix_pre, g_mix_post, g_ffn_pre, g_ffn_post, g_ple_post,
                 w_in, mu_shift, w_decay0, w_decay_up, a0, w_a_up, w_gate_up,
                 k_k, k_a, r_k, ln_x_w, ln_x_b, conv_w,
                 w_branch_a, w_branch_c, w_out,
                 w_ffn_gate, w_ffn_up, w_ffn_down, w_ple_in, w_ple_gate):
    Bsz, T, _ = x.shape
    f32 = jnp.float32
    h = rms_norm(x, g_mix_pre)
    z = h @ w_in
    zr = z[..., :D_SHIFT]
    z_conv = z[..., D_SHIFT:D_SHIFT + 3 * D_CONV]
    z_gate = z[..., D_SHIFT + 3 * D_CONV:]

    zr_prev = jnp.concatenate([shift_prev[:, None].astype(z.dtype), zr[:, :-1]], axis=1)
    zs = zr + mu_shift * (zr_prev - zr)
    r, k, v, wd, ad, gd = jnp.split(
        zs, [D_RWKV, 2 * D_RWKV, 3 * D_RWKV, 3 * D_RWKV + R_DECAY, 3 * D_RWKV + R_DECAY + R_AAA], axis=-1)
    r, k, v = r.astype(f32), k.astype(f32), v.astype(f32)
    w_log = -jax.nn.softplus(-(w_decay0 + jnp.tanh(wd) @ w_decay_up).astype(f32)) - 0.5
    decay = jnp.exp(-jnp.exp(w_log))
    a = jax.nn.sigmoid((a0 + ad @ w_a_up).astype(f32))
    g = (jax.nn.sigmoid(gd) @ w_gate_up).astype(f32)
    heads = lambda t: t.reshape(Bsz, T, N_HEADS, HEAD_DIM)
    kk = heads(k * k_k.astype(f32))
    kk = kk / jnp.maximum(jnp.linalg.norm(kk, axis=-1, keepdims=True), 1e-12)
    k = k * (1.0 + (a - 1.0) * k_a.astype(f32))
    rh, kh, vh, ah, dh = heads(r), heads(k), heads(v), heads(a), heads(decay)
    o, S_new = wkv7_scan(rh, dh, kh, vh, kk, ah, S0.astype(f32))
    o_mu = jnp.mean(o, axis=-1, keepdims=True)
    o_var = jnp.mean(jnp.square(o - o_mu), axis=-1, keepdims=True)
    o_n = ((o - o_mu) * lax.rsqrt(o_var + GN_EPS)).reshape(Bsz, T, D_RWKV)
    o_n = o_n * ln_x_w.astype(f32) + ln_x_b.astype(f32)
    bonus = jnp.sum(rh * kh * r_k.astype(f32), axis=-1, keepdims=True) * vh
    o_a = ((o_n + bonus.reshape(Bsz, T, D_RWKV)) * g).astype(x.dtype)
    y_a = o_a @ w_branch_a

    cb, cc, ch = jnp.split(z_conv, 3, axis=-1)
    u = cc * ch
    u_ext = jnp.concatenate([conv_buf.astype(u.dtype), u], axis=1)
    conv = conv_w[0] * u_ext[:, 0:T]
    for j in range(1, CONV_W):
        conv = conv + conv_w[j] * u_ext[:, j:j + T]
    y_c = (cb * conv) @ w_branch_c

    ga, gc = jnp.split(z_gate, 2, axis=-1)
    merged = jax.nn.sigmoid(ga) * y_a + jax.nn.sigmoid(gc) * y_c
    x = x + rms_norm(merged @ w_out, g_mix_post)

    h2 = rms_norm(x, g_ffn_pre)
    f = (jax.nn.silu(h2 @ w_ffn_gate) * (h2 @ w_ffn_up)) @ w_ffn_down
    x = x + rms_norm(f, g_ffn_post)

    e = (p.astype(x.dtype) @ w_ple_in) * jax.nn.sigmoid(rms_unit(x) @ w_ple_gate)
    x = x + rms_norm(e, g_ple_post)
    return x, S_new, zr[:, -1], u_ext[:, -(CONV_W - 1):]


def setup_inputs(seed: int = 0) -> dict:
    key = jax.random.key(seed)
    ks = iter(jax.random.split(key, 48))
    nrm = lambda shape, s: jax.random.normal(next(ks), shape, jnp.float32) * s
    gain = lambda shape: 1.0 + 0.01 * jax.random.normal(next(ks), shape, jnp.float32)
    L = DEPTH
    return {
        "x_prompt": nrm((BATCH, SEQ, D_MODEL), 1.0),
        "x_sample": nrm((DEC_BATCH, DEC_SEQ, D_MODEL), 1.0),
        "state_wkv": nrm((L, DEC_BATCH, N_HEADS, HEAD_DIM, HEAD_DIM), 0.1),
        "state_shift": nrm((L, DEC_BATCH, D_SHIFT), 1.0),
        "state_conv": nrm((L, DEC_BATCH, CONV_W - 1, D_CONV), 1.0),
        "p_prompt": nrm((L, BATCH, SEQ, PLE_DIM), 1.0),
        "p_sample": nrm((L, DEC_BATCH, DEC_SEQ, PLE_DIM), 1.0),
        "g_mix_pre": gain((L, D_MODEL)),
        "g_mix_post": gain((L, D_MODEL)),
        "g_ffn_pre": gain((L, D_MODEL)),
        "g_ffn_post": gain((L, D_MODEL)),
        "g_ple_post": gain((L, D_MODEL)),
        "w_in": nrm((L, D_MODEL, D_IN), D_MODEL ** -0.5),
        "mu_shift": jax.random.uniform(next(ks), (L, D_SHIFT), jnp.float32),
        "w_decay0": jax.random.uniform(next(ks), (L, D_RWKV), jnp.float32, -6.0, 1.0),
        "w_decay_up": nrm((L, R_DECAY, D_RWKV), 0.1 * R_DECAY ** -0.5),
        "a0": nrm((L, D_RWKV), 0.1),
        "w_a_up": nrm((L, R_AAA, D_RWKV), 0.5 * R_AAA ** -0.5),
        "w_gate_up": nrm((L, R_GATE, D_RWKV), R_GATE ** -0.5),
        "k_k": 0.85 + nrm((L, D_RWKV), 0.05),
        "k_a": 1.0 + nrm((L, D_RWKV), 0.05),
        "r_k": nrm((L, N_HEADS, HEAD_DIM), 0.1),
        "ln_x_w": gain((L, D_RWKV)),
        "ln_x_b": nrm((L, D_RWKV), 0.01),
        "conv_w": nrm((L, CONV_W, D_CONV), CONV_W ** -0.5),
        "w_branch_a": nrm((L, D_RWKV, D_MODEL), D_RWKV ** -0.5),
        "w_branch_c": nrm((L, D_CONV, D_MODEL), D_CONV ** -0.5),
        "w_out": nrm((L, D_MODEL, D_MODEL), D_MODEL ** -0.5),
        "w_ffn_gate": nrm((L, D_MODEL, D_FF), D_MODEL ** -0.5),
        "w_ffn_up": nrm((L, D_MODEL, D_FF), D_MODEL ** -0.5),
        "w_ffn_down": nrm((L, D_FF, D_MODEL), D_FF ** -0.5),
        "w_ple_in": nrm((L, PLE_DIM, D_MODEL), PLE_DIM ** -0.5),
        "w_ple_gate": nrm((L, D_MODEL, D_MODEL), D_MODEL ** -0.5),
    }


def reference(x_prompt, x_sample, state_wkv, state_shift, state_conv, p_prompt, p_sample,
              g_mix_pre, g_mix_post, g_ffn_pre, g_ffn_post, g_ple_post,
              w_in, mu_shift, w_decay0, w_decay_up, a0, w_a_up, w_gate_up,
              k_k, k_a, r_k, ln_x_w, ln_x_b, conv_w,
              w_branch_a, w_branch_c, w_out,
              w_ffn_gate, w_ffn_up, w_ffn_down, w_ple_in, w_ple_gate):
    Bp = x_prompt.shape[0]
    xp, xs = x_prompt, x_sample
    wkv_p, shift_p, conv_p = [], [], []
    wkv_s, shift_s, conv_s = [], [], []
    for i in range(DEPTH):
        weights = (g_mix_pre[i], g_mix_post[i], g_ffn_pre[i], g_ffn_post[i], g_ple_post[i],
                   w_in[i], mu_shift[i], w_decay0[i], w_decay_up[i], a0[i], w_a_up[i], w_gate_up[i],
                   k_k[i], k_a[i], r_k[i], ln_x_w[i], ln_x_b[i], conv_w[i],
                   w_branch_a[i], w_branch_c[i], w_out[i],
                   w_ffn_gate[i], w_ffn_up[i], w_ffn_down[i], w_ple_in[i], w_ple_gate[i])
        S0 = jnp.zeros((Bp, N_HEADS, HEAD_DIM, HEAD_DIM), jnp.float32)
        sh0 = jnp.zeros((Bp, D_SHIFT), xp.dtype)
        cb0 = jnp.zeros((Bp, CONV_W - 1, D_CONV), xp.dtype)
        xp, S_p, sh_p, cv_p = hybrid_layer(xp, p_prompt[i], S0, sh0, cb0, *weights)
        xs, S_s, sh_s, cv_s = hybrid_layer(xs, p_sample[i], state_wkv[i], state_shift[i], state_conv[i], *weights)
        wkv_p.append(S_p.astype(xp.dtype)); shift_p.append(sh_p); conv_p.append(cv_p)
        wkv_s.append(S_s.astype(state_wkv.dtype)); shift_s.append(sh_s.astype(state_shift.dtype)); conv_s.append(cv_s.astype(state_conv.dtype))
    new_wkv_prompt = jnp.stack(wkv_p)
    new_shift_prompt = jnp.stack(shift_p)
    new_conv_prompt = jnp.stack(conv_p)
    new_wkv_sample = jnp.stack(wkv_s)
    new_shift_sample = jnp.stack(shift_s)
    new_conv_sample = jnp.stack(conv_s)
    return (xp, xs, new_wkv_prompt, new_shift_prompt, new_conv_prompt,
            new_wkv_sample, new_shift_sample, new_conv_sample)
```

```python
import functools
import math

import jax
import jax.numpy as jnp
from jax import lax
from jax.experimental import pallas as pl
from jax.experimental.pallas import tpu as pltpu

HEAD_DIM = 64
LANES = 128
CHUNK = 64
R_DECAY = 64
R_AAA = 64
R_GATE = 128
CONV_W = 3
RMS_EPS = 1e-6
GN_EPS = 64e-5
VMEM_LIMIT_MB = 56

f32 = jnp.float32
bf16 = jnp.bfloat16


def _cparams(*sem):
    return pltpu.CompilerParams(dimension_semantics=sem, vmem_limit_bytes=VMEM_LIMIT_MB * 2**20)


def _tile(n, pref, mult=8):
    if n <= pref:
        return n
    for t in range(pref - pref % mult, 0, -mult):
        if n % t == 0:
            return t
    return n


def _dot(a, b):
    return jnp.dot(a, b, preferred_element_type=f32)


def _dot_nt(a, b):
    return lax.dot_general(a, b, (((1,), (1,)), ((), ())), preferred_element_type=f32)


def _dot_tn(a, b):
    return lax.dot_general(a, b, (((0,), (0,)), ((), ())), preferred_element_type=f32)


def _rms_unit(x):
    return x * lax.rsqrt(jnp.mean(x * x, axis=-1, keepdims=True) + RMS_EPS)


def _inproj_kernel(x_ref, g_ref, w_ref, o_ref, h_ref):
    @pl.when(pl.program_id(1) == 0)
    def _():
        h_ref[...] = (_rms_unit(x_ref[...]) * g_ref[...]).astype(bf16)
    o_ref[...] = _dot(h_ref[...], w_ref[...])


def _inproj(x, g, w):
    M, D = x.shape
    NZ = w.shape[1]
    tm, tn = _tile(M, 1024), _tile(NZ, 1024, LANES)
    return pl.pallas_call(
        _inproj_kernel,
        grid=(M // tm, NZ // tn),
        in_specs=[pl.BlockSpec((tm, D), lambda i, j: (i, 0)),
                  pl.BlockSpec((1, D), lambda i, j: (0, 0)),
                  pl.BlockSpec((D, tn), lambda i, j: (0, j))],
        out_specs=pl.BlockSpec((tm, tn), lambda i, j: (i, j)),
        out_shape=jax.ShapeDtypeStruct((M, NZ), f32),
        scratch_shapes=[pltpu.VMEM((tm, D), bf16)],
        compiler_params=_cparams("parallel", "arbitrary"),
        name="inproj")(x, g, w)


def _merge_kernel(oa_ref, ci_ref, wa_ref, wc_ref, ga_ref, gc_ref, o_ref):
    ya = _dot(oa_ref[...], wa_ref[...])
    yc = _dot(ci_ref[...], wc_ref[...])
    o_ref[...] = (jax.nn.sigmoid(ga_ref[...]) * ya + jax.nn.sigmoid(gc_ref[...]) * yc).astype(bf16)


def _merge(oa, ci, wa, wc, z, gate_off):
    M, DC = oa.shape
    D = wa.shape[1]
    tm, tn = _tile(M, 512), _tile(DC, 1024, LANES)
    ga0, gc0 = gate_off // tn, (gate_off + D) // tn
    return pl.pallas_call(
        _merge_kernel,
        grid=(M // tm, D // tn),
        in_specs=[pl.BlockSpec((tm, DC), lambda i, j: (i, 0)),
                  pl.BlockSpec((tm, DC), lambda i, j: (i, 0)),
                  pl.BlockSpec((DC, tn), lambda i, j: (0, j)),
                  pl.BlockSpec((DC, tn), lambda i, j: (0, j)),
                  pl.BlockSpec((tm, tn), lambda i, j: (i, ga0 + j)),
                  pl.BlockSpec((tm, tn), lambda i, j: (i, gc0 + j))],
        out_specs=pl.BlockSpec((tm, tn), lambda i, j: (i, j)),
        out_shape=jax.ShapeDtypeStruct((M, D), bf16),
        compiler_params=_cparams("parallel", "arbitrary"),
        name="merge")(oa, ci, wa, wc, z, z)


def _outproj_kernel(m_ref, w_ref, x_ref, g_ref, o_ref):
    y = _dot(m_ref[...], w_ref[...])
    o_ref[...] = x_ref[...] + _rms_unit(y) * g_ref[...]


def _outproj(m, w, x, g):
    M, D = x.shape
    tm = _tile(M, 512)
    return pl.pallas_call(
        _outproj_kernel,
        grid=(M // tm,),
        in_specs=[pl.BlockSpec((tm, D), lambda i: (i, 0)),
                  pl.BlockSpec((D, D), lambda i: (0, 0)),
                  pl.BlockSpec((tm, D), lambda i: (i, 0)),
                  pl.BlockSpec((1, D), lambda i: (0, 0))],
        out_specs=pl.BlockSpec((tm, D), lambda i: (i, 0)),
        out_shape=jax.ShapeDtypeStruct((M, D), f32),
        compiler_params=_cparams("parallel"),
        name="outproj")(m, w, x, g)


def _ffn_kernel(x_ref, g1_ref, wg_ref, wu_ref, wd_ref, g2_ref, o_ref, h_ref, acc_ref):
    j = pl.program_id(1)

    @pl.when(j == 0)
    def _():
        h_ref[...] = (_rms_unit(x_ref[...]) * g1_ref[...]).astype(bf16)
        acc_ref[...] = jnp.zeros_like(acc_ref)

    h = h_ref[...]
    a = _dot(h, wg_ref[...])
    u = _dot(h, wu_ref[...])
    act = (a * jax.nn.sigmoid(a) * u).astype(bf16)
    acc_ref[...] += _dot(act, wd_ref[...])

    @pl.when(j == pl.num_programs(1) - 1)
    def _():
        o_ref[...] = x_ref[...] + _rms_unit(acc_ref[...]) * g2_ref[...]


def _ffn(x, g1, wg, wu, wd, g2):
    M, D = x.shape
    F = wg.shape[1]
    tm, tf = _tile(M, 512), _tile(F, 512, LANES)
    return pl.pallas_call(
        _ffn_kernel,
        grid=(M // tm, F // tf),
        in_specs=[pl.BlockSpec((tm, D), lambda i, j: (i, 0)),
                  pl.BlockSpec((1, D), lambda i, j: (0, 0)),
                  pl.BlockSpec((D, tf), lambda i, j: (0, j)),
                  pl.BlockSpec((D, tf), lambda i, j: (0, j)),
                  pl.BlockSpec((tf, D), lambda i, j: (j, 0)),
                  pl.BlockSpec((1, D), lambda i, j: (0, 0))],
        out_specs=pl.BlockSpec((tm, D), lambda i, j: (i, 0)),
        out_shape=jax.ShapeDtypeStruct((M, D), f32),
        scratch_shapes=[pltpu.VMEM((tm, D), bf16), pltpu.VMEM((tm, D), f32)],
        compiler_params=_cparams("parallel", "arbitrary"),
        name="ffn")(x, g1, wg, wu, wd, g2)


def _ple_kernel(x_ref, p_ref, wi_ref, wg_ref, g_ref, o_ref):
    x = x_ref[...]
    e = _dot(p_ref[...].astype(bf16), wi_ref[...]) * jax.nn.sigmoid(
        _dot(_rms_unit(x).astype(bf16), wg_ref[...]))
    o_ref[...] = x + _rms_unit(e) * g_ref[...]


def _ple(x, p, wi, wg, g):
    M, D = x.shape
    P = p.shape[1]
    tm = _tile(M, 512)
    return pl.pallas_call(
        _ple_kernel,
        grid=(M // tm,),
        in_specs=[pl.BlockSpec((tm, D), lambda i: (i, 0)),
                  pl.BlockSpec((tm, P), lambda i: (i, 0)),
                  pl.BlockSpec((P, D), lambda i: (0, 0)),
                  pl.BlockSpec((D, D), lambda i: (0, 0)),
                  pl.BlockSpec((1, D), lambda i: (0, 0))],
        out_specs=pl.BlockSpec((tm, D), lambda i: (i, 0)),
        out_shape=jax.ShapeDtypeStruct((M, D), f32),
        compiler_params=_cparams("parallel"),
        name="ple")(x, p, wi, wg, g)


def _conv_prompt_kernel(cb_ref, cc_ref, ch_ref, ccp_ref, chp_ref, w_ref, o_ref, st_ref, *, tiles_per_seq):
    tm = cb_ref.shape[0]
    first = pl.program_id(0) % tiles_per_seq == 0
    u = cc_ref[...] * ch_ref[...]
    up = ccp_ref[...] * chp_ref[...]
    p0 = jnp.where(first, 0.0, up[6:7])
    p1 = jnp.where(first, 0.0, up[7:8])
    row = lax.broadcasted_iota(jnp.int32, u.shape, 0)
    u1 = jnp.where(row == 0, p1, pltpu.roll(u, 1, 0))
    u2 = jnp.where(row == 0, p0, jnp.where(row == 1, p1, pltpu.roll(u, 2, 0)))
    conv = w_ref[0:1] * u2 + w_ref[1:2] * u1 + w_ref[2:3] * u
    o_ref[...] = (cb_ref[...] * conv).astype(bf16)
    st_ref[0] = u[tm - (CONV_W - 1):tm]


def _conv_prompt(z, conv_w, conv_off, B, T):
    M = z.shape[0]
    DC = conv_w.shape[1]
    tm = _tile(T, 512)
    c0 = conv_off // DC
    prev = lambda i: (jnp.maximum(i * (tm // 8) - 1, 0))
    return pl.pallas_call(
        functools.partial(_conv_prompt_kernel, tiles_per_seq=T // tm),
        grid=(M // tm,),
        in_specs=[pl.BlockSpec((tm, DC), lambda i: (i, c0)),
                  pl.BlockSpec((tm, DC), lambda i: (i, c0 + 1)),
                  pl.BlockSpec((tm, DC), lambda i: (i, c0 + 2)),
                  pl.BlockSpec((8, DC), lambda i: (prev(i), c0 + 1)),
                  pl.BlockSpec((8, DC), lambda i: (prev(i), c0 + 2)),
                  pl.BlockSpec((CONV_W, DC), lambda i: (0, 0))],
        out_specs=[pl.BlockSpec((tm, DC), lambda i: (i, 0)),
                   pl.BlockSpec((1, CONV_W - 1, DC), lambda i: (i // (T // tm), 0, 0))],
        out_shape=[jax.ShapeDtypeStruct((M, DC), bf16),
                   jax.ShapeDtypeStruct((B, CONV_W - 1, DC), f32)],
        compiler_params=_cparams("arbitrary"),
        name="conv_prompt")(z, z, z, z, z, conv_w)


def _conv_sample_kernel(cb_ref, cc_ref, ch_ref, b0_ref, b1_ref, w_ref, o_ref, u_ref):
    u = cc_ref[...] * ch_ref[...]
    conv = w_ref[0:1] * b0_ref[...] + w_ref[1:2] * b1_ref[...] + w_ref[2:3] * u
    o_ref[...] = (cb_ref[...] * conv).astype(bf16)
    u_ref[...] = u


def _conv_sample(z, conv_w, conv_off, buf0, buf1):
    M = z.shape[0]
    DC = conv_w.shape[1]
    c0 = conv_off // DC
    col = lambda c: pl.BlockSpec((M, DC), lambda i: (0, c))
    return pl.pallas_call(
        _conv_sample_kernel,
        grid=(1,),
        in_specs=[col(c0), col(c0 + 1), col(c0 + 2), col(0), col(0),
                  pl.BlockSpec((CONV_W, DC), lambda i: (0, 0))],
        out_specs=[col(0), col(0)],
        out_shape=[jax.ShapeDtypeStruct((M, DC), bf16), jax.ShapeDtypeStruct((M, DC), f32)],
        compiler_params=_cparams("arbitrary"),
        name="conv_sample")(z, z, z, buf0, buf1, conv_w)


def _head_ones():
    r = lax.broadcasted_iota(jnp.int32, (LANES, LANES), 0) >> 6
    c = lax.broadcasted_iota(jnp.int32, (LANES, LANES), 1) >> 6
    return (r == c).astype(bf16)


def _head_sum(x, ones):
    hi = x.astype(bf16)
    lo = (x - hi.astype(f32)).astype(bf16)
    return _dot(hi, ones) + _dot(lo, ones)


def _wkv_prep(rs, ks, vs, ls, wd0, a0, k_k, k_a, r_k, wda, wgu, ones):
    lane = lax.broadcasted_iota(jnp.int32, rs.shape, 1)
    l0 = ls[:, :LANES]
    lin = jnp.where(lane < R_DECAY, jnp.tanh(l0), l0).astype(bf16)
    da = _dot(lin, wda)
    lw = -math.exp(-0.5) * jax.nn.sigmoid(wd0 + da[:, :LANES])
    a = jax.nn.sigmoid(a0 + da[:, LANES:])
    g = _dot(jax.nn.sigmoid(ls[:, LANES:]).astype(bf16), wgu)
    kkr = ks * k_k
    kap = kkr / jnp.maximum(jnp.sqrt(_head_sum(kkr * kkr, ones)), 1e-12)
    kt = ks * (1.0 + (a - 1.0) * k_a)
    bonus = _head_sum(rs * kt * r_k, ones) * vs
    return lw, a, kap, kt, g, bonus


def _wkv_post(o, bonus, g, ln_w, ln_b, ones):
    inv = 1.0 / HEAD_DIM
    d = o - _head_sum(o, ones) * inv
    var = _head_sum(d * d, ones) * inv
    on = d * lax.rsqrt(var + GN_EPS) * ln_w + ln_b
    return ((on + bonus) * g).astype(bf16)


def _wkv_prompt_kernel(zr_ref, zk_ref, zv_ref, zl_ref, mur_ref, muk_ref, muv_ref, mul_ref,
                       wd0_ref, a0_ref, kk_ref, ka_ref, rk_ref, lnw_ref, lnb_ref, wda_ref, wgu_ref,
                       o_ref, s_ref, st_ref, pr_ref, pk_ref, pv_ref, pl_ref, o_scr):
    t = pl.program_id(2)
    tb = zr_ref.shape[0]

    @pl.when(t == 0)
    def _():
        st_ref[...] = jnp.zeros_like(st_ref)
        pr_ref[...] = jnp.zeros_like(pr_ref)
        pk_ref[...] = jnp.zeros_like(pk_ref)
        pv_ref[...] = jnp.zeros_like(pv_ref)
        pl_ref[...] = jnp.zeros_like(pl_ref)

    def shifted(x_ref, p_ref, mu_ref):
        x = x_ref[...]
        row = lax.broadcasted_iota(jnp.int32, x.shape, 0)
        xp = jnp.where(row == 0, p_ref[...], pltpu.roll(x, 1, 0))
        p_ref[...] = x[tb - 1:tb]
        return x + mu_ref[...] * (xp - x)

    rs = shifted(zr_ref, pr_ref, mur_ref)
    ks = shifted(zk_ref, pk_ref, muk_ref)
    vs = shifted(zv_ref, pv_ref, muv_ref)
    ls = shifted(zl_ref, pl_ref, mul_ref)
    ones = _head_ones()
    lw, a, kap, kt, g, bonus = _wkv_prep(rs, ks, vs, ls, wd0_ref[...], a0_ref[...], kk_ref[...],
                                         ka_ref[...], rk_ref[...], wda_ref[0], wgu_ref[...], ones)
    b = kap * a

    ri = lax.broadcasted_iota(jnp.int32, (tb, tb), 0)
    ci = lax.broadcasted_iota(jnp.int32, (tb, tb), 1)
    tri = (((ri >> 6) == (ci >> 6)) & (ci <= ri)).astype(bf16)
    h1 = lw.astype(bf16)
    r1 = lw - h1.astype(f32)
    h2 = r1.astype(bf16)
    h3 = (r1 - h2.astype(f32)).astype(bf16)
    cum = _dot(tri, h1) + _dot(tri, h2) + _dot(tri, h3)

    rr = lax.broadcasted_iota(jnp.int32, (LANES, LANES), 0)
    cc = lax.broadcasted_iota(jnp.int32, (LANES, LANES), 1)
    same_head = (rr >> 6) == (cc >> 6)
    tt, ss = rr & (CHUNK - 1), cc & (CHUNK - 1)
    strict = same_head & (ss < tt)
    incl = same_head & (ss <= tt)
    eye = rr == cc

    def level_mask(k):
        rb, cb = tt >> k, ss >> k
        return same_head & ((rb >> 1) == (cb >> 1)) & ((rb & 1) == 1) & ((cb & 1) == 0)

    def pair(x):
        return jnp.where(same_head, jnp.concatenate([x, x], axis=0), 0.0)

    for c in range(tb // CHUNK):
        sl = slice(c * CHUNK, (c + 1) * CHUNK)
        cm = cum[sl]
        c_last = cum[(c + 1) * CHUNK - 1:(c + 1) * CHUNK]
        e_p, e_m = jnp.exp(cm), jnp.exp(-cm)
        e_x, e_d = jnp.exp(cm - lw[sl]), jnp.exp(c_last - cm)
        kh = pair(kap[sl] * e_x).astype(bf16)
        rh = pair(rs[sl] * e_p)
        vb = pair(vs[sl]).astype(bf16)
        bd = pair(b[sl] * e_d).astype(bf16)
        kd = pair(kt[sl] * e_d).astype(bf16)
        x = jnp.concatenate([kh, rh.astype(bf16)], axis=0)
        yb = b[sl] * e_m
        yk = kt[sl] * e_m
        a_b = _dot_nt(x, jnp.concatenate([yb, yb], axis=0).astype(bf16))
        a_k = _dot_nt(x, jnp.concatenate([yk, yk], axis=0).astype(bf16))
        low = jnp.where(strict, a_b[:LANES], 0.0)
        akk = jnp.where(strict, a_k[:LANES], 0.0).astype(bf16)
        arb = jnp.where(incl, a_b[LANES:], 0.0).astype(bf16)
        ark = jnp.where(incl, a_k[LANES:], 0.0).astype(bf16)

        inv = jnp.where(eye, 1.0, 0.0) - jnp.where(level_mask(0), low, 0.0)
        for k in range(1, 6):
            ib = inv.astype(bf16)
            mk = jnp.where(level_mask(k), low, 0.0).astype(bf16)
            inv = inv - _dot(_dot(ib, mk).astype(bf16), ib)
        w = inv.astype(bf16)

        p = _dot(akk, vb).astype(bf16)
        uk = _dot(w, jnp.concatenate([p, kh], axis=1)).astype(bf16)
        q = _dot(arb, uk)
        o0 = _dot(ark, vb) - q[:, :LANES]
        rbar = (rh - q[:, LANES:]).astype(bf16)
        z = _dot_tn(uk, bd)
        g0t = _dot_tn(vb, kd) - z[:LANES]
        phit = jnp.where(eye, jnp.exp(c_last), 0.0) - z[LANES:]
        s = st_ref[...].astype(bf16)
        o = _dot_nt(rbar, s) + o0
        st_ref[...] = _dot(s, phit.astype(bf16)) + g0t
        o_scr[sl, :] = o[:CHUNK] + o[CHUNK:]

    o_ref[...] = _wkv_post(o_scr[...], bonus, g, lnw_ref[...], lnb_ref[...], ones)

    @pl.when(t == pl.num_programs(2) - 1)
    def _():
        s_ref[0, 0] = st_ref[...]


def _wkv_prompt(z, lp, B, T):
    DR = lp["ln_x_w"].shape[1]
    nhp = DR // LANES
    tb = _tile(T, 256, CHUNK)
    nt = T // tb
    lora_blk = 3 * DR // (2 * LANES)
    zcol = lambda off: pl.BlockSpec((tb, LANES), lambda bb, hp, t: (bb * nt + t, off + hp))
    vec = lambda off: pl.BlockSpec((1, LANES), lambda bb, hp, t: (0, off + hp))
    in_specs = [zcol(0), zcol(nhp), zcol(2 * nhp),
                pl.BlockSpec((tb, 2 * LANES), lambda bb, hp, t: (bb * nt + t, lora_blk)),
                vec(0), vec(nhp), vec(2 * nhp),
                pl.BlockSpec((1, 2 * LANES), lambda bb, hp, t: (0, lora_blk)),
                vec(0), vec(0), vec(0), vec(0), vec(0), vec(0), vec(0),
                pl.BlockSpec((1, LANES, 2 * LANES), lambda bb, hp, t: (hp, 0, 0)),
                pl.BlockSpec((R_GATE, LANES), lambda bb, hp, t: (0, hp))]
    return pl.pallas_call(
        _wkv_prompt_kernel,
        grid=(B, nhp, nt),
        in_specs=in_specs,
        out_specs=[pl.BlockSpec((tb, LANES), lambda bb, hp, t: (bb * nt + t, hp)),
                   pl.BlockSpec((1, 1, LANES, LANES), lambda bb, hp, t: (bb, hp, 0, 0))],
        out_shape=[jax.ShapeDtypeStruct((B * T, DR), bf16),
                   jax.ShapeDtypeStruct((B, nhp, LANES, LANES), f32)],
        scratch_shapes=[pltpu.VMEM((LANES, LANES), f32),
                        pltpu.VMEM((1, LANES), f32), pltpu.VMEM((1, LANES), f32),
                        pltpu.VMEM((1, LANES), f32), pltpu.VMEM((1, 2 * LANES), f32),
                        pltpu.VMEM((tb, LANES), f32)],
        compiler_params=_cparams("parallel", "parallel", "arbitrary"),
        name="wkv_prompt")(z, z, z, z, lp["mu"], lp["mu"], lp["mu"], lp["mu"],
                           lp["w_decay0"], lp["a0"], lp["k_k"], lp["k_a"], lp["r_k"],
                           lp["ln_x_w"], lp["ln_x_b"], lp["wda"], lp["wgu"])


def _wkv_sample_prep_kernel(zr_ref, zk_ref, zv_ref, zl_ref, sr_ref, sk_ref, sv_ref, sl_ref,
                            mur_ref, muk_ref, muv_ref, mul_ref,
                            wd0_ref, a0_ref, kk_ref, ka_ref, rk_ref, wda_ref, wgu_ref,
                            r_out, w_out, kt_out, v_out, kap_out, b_out, g_out, bonus_out):
    def shifted(x_ref, p_ref, mu_ref):
        x = x_ref[...]
        return x + mu_ref[...] * (p_ref[...] - x)

    rs = shifted(zr_ref, sr_ref, mur_ref)
    ks = shifted(zk_ref, sk_ref, muk_ref)
    vs = shifted(zv_ref, sv_ref, muv_ref)
    ls = shifted(zl_ref, sl_ref, mul_ref)
    ones = _head_ones()
    lw, a, kap, kt, g, bonus = _wkv_prep(rs, ks, vs, ls, wd0_ref[...], a0_ref[...], kk_ref[...],
                                         ka_ref[...], rk_ref[...], wda_ref[0], wgu_ref[...], ones)
    r_out[...] = rs
    w_out[...] = jnp.exp(lw)
    kt_out[...] = kt
    v_out[...] = vs
    kap_out[...] = kap
    b_out[...] = kap * a
    g_out[...] = g
    bonus_out[...] = bonus


def _wkv_sample_prep(z, shift_prev, lp):
    M = z.shape[0]
    DR = lp["ln_x_w"].shape[1]
    nhp = DR // LANES
    lora_blk = 3 * DR // (2 * LANES)
    col = lambda off: pl.BlockSpec((M, LANES), lambda hp: (0, off + hp))
    vec = lambda off: pl.BlockSpec((1, LANES), lambda hp: (0, off + hp))
    wide = lambda rows: pl.BlockSpec((rows, 2 * LANES), lambda hp: (0, lora_blk))
    in_specs = [col(0), col(nhp), col(2 * nhp), wide(M),
                col(0), col(nhp), col(2 * nhp), wide(M),
                vec(0), vec(nhp), vec(2 * nhp), wide(1),
                vec(0), vec(0), vec(0), vec(0), vec(0),
                pl.BlockSpec((1, LANES, 2 * LANES), lambda hp: (hp, 0, 0)),
                pl.BlockSpec((R_GATE, LANES), lambda hp: (0, hp))]
    return pl.pallas_call(
        _wkv_sample_prep_kernel,
        grid=(nhp,),
        in_specs=in_specs,
        out_specs=[col(0)] * 8,
        out_shape=[jax.ShapeDtypeStruct((M, DR), f32)] * 8,
        compiler_params=_cparams("parallel"),
        name="wkv_sample_prep")(z, z, z, z, shift_prev, shift_prev, shift_prev, shift_prev,
                                lp["mu"], lp["mu"], lp["mu"], lp["mu"],
                                lp["w_decay0"], lp["a0"], lp["k_k"], lp["k_a"], lp["r_k"],
                                lp["wda"], lp["wgu"])


def _wkv_step_kernel(s_ref, w_ref, kap_ref, b_ref, kt_ref, r_ref, v_ref, so_ref, o_ref):
    s = s_ref[...]
    shape = s.shape
    eye = lax.broadcasted_iota(jnp.int32, shape, 1) == lax.broadcasted_iota(jnp.int32, shape, 2)
    sa = jnp.sum(s * kap_ref[...], axis=2, keepdims=True)
    v_col = jnp.sum(jnp.where(eye, v_ref[...], 0.0), axis=2, keepdims=True)
    s_new = s * w_ref[...] - sa * b_ref[...] + v_col * kt_ref[...]
    so_ref[...] = s_new
    o_col = jnp.sum(s_new * r_ref[...], axis=2, keepdims=True)
    o_ref[...] = jnp.sum(jnp.where(eye, o_col, 0.0), axis=1, keepdims=True)


def _wkv_step(s, w, kap, b, kt, r, v):
    G = s.shape[0]
    gt = _tile(G, 128)
    mat = pl.BlockSpec((gt, HEAD_DIM, HEAD_DIM), lambda i: (i, 0, 0))
    row = pl.BlockSpec((gt, 1, HEAD_DIM), lambda i: (i, 0, 0))
    return pl.pallas_call(
        _wkv_step_kernel,
        grid=(G // gt,),
        in_specs=[mat] + [row] * 6,
        out_specs=[mat, row],
        out_shape=[jax.ShapeDtypeStruct(s.shape, f32), jax.ShapeDtypeStruct((G, 1, HEAD_DIM), f32)],
        compiler_params=_cparams("parallel"),
        name="wkv_step")(s, w, kap, b, kt, r, v)


def _wkv_sample_post_kernel(o_ref, bonus_ref, g_ref, lnw_ref, lnb_ref, oa_ref):
    oa_ref[...] = _wkv_post(o_ref[...], bonus_ref[...], g_ref[...], lnw_ref[...], lnb_ref[...], _head_ones())


def _wkv_sample_post(o, bonus, g, lp):
    M, DR = o.shape
    col = pl.BlockSpec((M, LANES), lambda hp: (0, hp))
    vec = pl.BlockSpec((1, LANES), lambda hp: (0, hp))
    return pl.pallas_call(
        _wkv_sample_post_kernel,
        grid=(DR // LANES,),
        in_specs=[col, col, col, vec, vec],
        out_specs=col,
        out_shape=jax.ShapeDtypeStruct((M, DR), bf16),
        compiler_params=_cparams("parallel"),
        name="wkv_sample_post")(o, bonus, g, lp["ln_x_w"], lp["ln_x_b"])


def _layer_params(i, D, DR, d_shift_p, g_mix_pre, g_mix_post, g_ffn_pre, g_ffn_post, g_ple_post,
                  w_in, mu_shift, w_decay0, w_decay_up, a0, w_a_up, w_gate_up, k_k, k_a, r_k,
                  ln_x_w, ln_x_b, conv_w, w_branch_a, w_branch_c, w_out,
                  w_ffn_gate, w_ffn_up, w_ffn_down, w_ple_in, w_ple_gate):
    d_shift = mu_shift.shape[1]
    nhp = DR // LANES
    row = lambda v: v[i].reshape(1, -1)
    w_in_p = jnp.concatenate([w_in[i][:, :d_shift], jnp.zeros((D, d_shift_p - d_shift), f32),
                              w_in[i][:, d_shift:]], axis=1).astype(bf16)
    wdu = w_decay_up[i].reshape(R_DECAY, nhp, LANES).transpose(1, 0, 2)
    wau = w_a_up[i].reshape(R_AAA, nhp, LANES).transpose(1, 0, 2)
    wda = jnp.concatenate([jnp.concatenate([wdu, jnp.zeros_like(wdu)], axis=2),
                           jnp.concatenate([jnp.zeros_like(wau), wau], axis=2)], axis=1).astype(bf16)
    return dict(
        g_mix_pre=row(g_mix_pre), g_mix_post=row(g_mix_post), g_ffn_pre=row(g_ffn_pre),
        g_ffn_post=row(g_ffn_post), g_ple_post=row(g_ple_post),
        w_in=w_in_p, mu=row(mu_shift), w_decay0=row(w_decay0), a0=row(a0), k_k=row(k_k), k_a=row(k_a),
        r_k=row(r_k), ln_x_w=row(ln_x_w), ln_x_b=row(ln_x_b), wda=wda, wgu=w_gate_up[i].astype(bf16),
        conv_w=conv_w[i], w_branch_a=w_branch_a[i].astype(bf16), w_branch_c=w_branch_c[i].astype(bf16),
        w_out=w_out[i].astype(bf16), w_ffn_gate=w_ffn_gate[i].astype(bf16),
        w_ffn_up=w_ffn_up[i].astype(bf16), w_ffn_down=w_ffn_down[i].astype(bf16),
        w_ple_in=w_ple_in[i].astype(bf16), w_ple_gate=w_ple_gate[i].astype(bf16))


def _mix_tail(x, p, z, oa, ci, lp, gate_off):
    merged = _merge(oa, ci, lp["w_branch_a"], lp["w_branch_c"], z, gate_off)
    x = _outproj(merged, lp["w_out"], x, lp["g_mix_post"])
    x = _ffn(x, lp["g_ffn_pre"], lp["w_ffn_gate"], lp["w_ffn_up"], lp["w_ffn_down"], lp["g_ffn_post"])
    return _ple(x, p, lp["w_ple_in"], lp["w_ple_gate"], lp["g_ple_post"])


def kernel(x_prompt, x_sample, state_wkv, state_shift, state_conv, p_prompt, p_sample, g_mix_pre, g_mix_post, g_ffn_pre, g_ffn_post, g_ple_post, w_in, mu_shift, w_decay0, w_decay_up, a0, w_a_up, w_gate_up, k_k, k_a, r_k, ln_x_w, ln_x_b, conv_w, w_branch_a, w_branch_c, w_out, w_ffn_gate, w_ffn_up, w_ffn_down, w_ple_in, w_ple_gate):
    B, T, D = x_prompt.shape
    DB, DT, _ = x_sample.shape
    depth = w_in.shape[0]
    DR = w_decay0.shape[1]
    DC = conv_w.shape[2]
    H = DR // HEAD_DIM
    d_shift = mu_shift.shape[1]
    assert DT == 1 and T % CHUNK == 0 and T >= CONV_W - 1
    assert DR % (2 * LANES) == 0 and DC == DR and d_shift == 3 * DR + R_DECAY + R_AAA + R_GATE
    d_shift_p = -(-d_shift // DC) * DC
    conv_off = d_shift_p
    gate_off = d_shift_p + 3 * DC

    xp = x_prompt.reshape(B * T, D)
    xs = x_sample.reshape(DB, D)
    outs = [[] for _ in range(6)]
    for i in range(depth):
        lp = _layer_params(i, D, DR, d_shift_p, g_mix_pre, g_mix_post, g_ffn_pre, g_ffn_post, g_ple_post,
                           w_in, mu_shift, w_decay0, w_decay_up, a0, w_a_up, w_gate_up, k_k, k_a,
                           r_k.reshape(depth, DR), ln_x_w, ln_x_b, conv_w, w_branch_a, w_branch_c, w_out,
                           w_ffn_gate, w_ffn_up, w_ffn_down, w_ple_in, w_ple_gate)

        zp = _inproj(xp, lp["g_mix_pre"], lp["w_in"])
        oa, s_pair = _wkv_prompt(zp, lp, B, T)
        ci, conv_p = _conv_prompt(zp, lp["conv_w"], conv_off, B, T)
        xp = _mix_tail(xp, p_prompt[i].reshape(B * T, -1), zp, oa, ci, lp, gate_off)
        s_p = jnp.stack([s_pair[:, :, :HEAD_DIM, :HEAD_DIM], s_pair[:, :, HEAD_DIM:, HEAD_DIM:]], axis=2)
        outs[0].append(s_p.reshape(B, H, HEAD_DIM, HEAD_DIM))
        outs[1].append(zp.reshape(B, T, -1)[:, T - 1, :d_shift])
        outs[2].append(conv_p)

        zs = _inproj(xs, lp["g_mix_pre"], lp["w_in"])
        r, w, kt, v, kap, b, g, bonus = _wkv_sample_prep(zs, state_shift[i], lp)
        rows = lambda a_: a_.reshape(DB * H, 1, HEAD_DIM)
        s_new, o = _wkv_step(state_wkv[i].reshape(DB * H, HEAD_DIM, HEAD_DIM),
                             rows(w), rows(kap), rows(b), rows(kt), rows(r), rows(v))
        oa = _wkv_sample_post(o.reshape(DB, DR), bonus, g, lp)
        ci, u = _conv_sample(zs, lp["conv_w"], conv_off, state_conv[i][:, 0], state_conv[i][:, 1])
        xs = _mix_tail(xs, p_sample[i].reshape(DB, -1), zs, oa, ci, lp, gate_off)
        outs[3].append(s_new.reshape(DB, H, HEAD_DIM, HEAD_DIM))
        outs[4].append(zs[:, :d_shift])
        outs[5].append(jnp.stack([state_conv[i][:, 1], u], axis=1))

    return (xp.reshape(B, T, D), xs.reshape(DB, 1, D)) + tuple(jnp.stack(o) for o in outs)
```

```python
import functools
import math

import jax
import jax.numpy as jnp
from jax import lax
from jax.experimental import pallas as pl
from jax.experimental.pallas import tpu as pltpu

HEAD_DIM = 64
LANES = 128
CHUNK = 64
R_DECAY = 64
R_AAA = 64
R_GATE = 128
CONV_W = 3
RMS_EPS = 1e-6
GN_EPS = 64e-5
VMEM_LIMIT_MB = 56

f32 = jnp.float32
bf16 = jnp.bfloat16


def _cparams(*sem):
    return pltpu.CompilerParams(dimension_semantics=sem, vmem_limit_bytes=VMEM_LIMIT_MB * 2**20)


def _tile(n, pref, mult=8):
    if n <= pref:
        return n
    for t in range(pref - pref % mult, 0, -mult):
        if n % t == 0:
            return t
    return n


def _dot(a, b):
    return jnp.dot(a, b, preferred_element_type=f32)


def _dot_nt(a, b):
    return lax.dot_general(a, b, (((1,), (1,)), ((), ())), preferred_element_type=f32)


def _dot_tn(a, b):
    return lax.dot_general(a, b, (((0,), (0,)), ((), ())), preferred_element_type=f32)


def _rms_unit(x):
    return x * lax.rsqrt(jnp.mean(x * x, axis=-1, keepdims=True) + RMS_EPS)


def _inproj_kernel(x_ref, g_ref, w_ref, o_ref, h_ref):
    @pl.when(pl.program_id(1) == 0)
    def _():
        h_ref[...] = (_rms_unit(x_ref[...]) * g_ref[...]).astype(bf16)
    o_ref[...] = _dot(h_ref[...], w_ref[...])


def _inproj(x, g, w, li):
    M, D = x.shape
    NZ = w.shape[2]
    tm, tn = _tile(M, 1024), _tile(NZ, 1024, LANES)
    return pl.pallas_call(
        _inproj_kernel,
        grid=(M // tm, NZ // tn),
        in_specs=[pl.BlockSpec((tm, D), lambda i, j: (i, 0)),
                  pl.BlockSpec((1, D), lambda i, j: (0, 0)),
                  pl.BlockSpec((None, D, tn), lambda i, j: (li, 0, j))],
        out_specs=pl.BlockSpec((tm, tn), lambda i, j: (i, j)),
        out_shape=jax.ShapeDtypeStruct((M, NZ), f32),
        scratch_shapes=[pltpu.VMEM((tm, D), bf16)],
        compiler_params=_cparams("parallel", "arbitrary"),
        name="inproj")(x, g, w)


def _merge_kernel(oa_ref, ci_ref, wa_ref, wc_ref, ga_ref, gc_ref, o_ref):
    ya = _dot(oa_ref[...], wa_ref[...])
    yc = _dot(ci_ref[...], wc_ref[...])
    o_ref[...] = (jax.nn.sigmoid(ga_ref[...]) * ya + jax.nn.sigmoid(gc_ref[...]) * yc).astype(bf16)


def _merge(oa, ci, wa, wc, z, gate_off, li):
    M, DC = oa.shape
    D = wa.shape[2]
    tm, tn = _tile(M, 512), _tile(DC, 1024, LANES)
    ga0, gc0 = gate_off // tn, (gate_off + D) // tn
    return pl.pallas_call(
        _merge_kernel,
        grid=(M // tm, D // tn),
        in_specs=[pl.BlockSpec((tm, DC), lambda i, j: (i, 0)),
                  pl.BlockSpec((tm, DC), lambda i, j: (i, 0)),
                  pl.BlockSpec((None, DC, tn), lambda i, j: (li, 0, j)),
                  pl.BlockSpec((None, DC, tn), lambda i, j: (li, 0, j)),
                  pl.BlockSpec((tm, tn), lambda i, j: (i, ga0 + j)),
                  pl.BlockSpec((tm, tn), lambda i, j: (i, gc0 + j))],
        out_specs=pl.BlockSpec((tm, tn), lambda i, j: (i, j)),
        out_shape=jax.ShapeDtypeStruct((M, D), bf16),
        compiler_params=_cparams("parallel", "arbitrary"),
        name="merge")(oa, ci, wa, wc, z, z)


def _outproj_kernel(m_ref, w_ref, x_ref, g_ref, o_ref):
    y = _dot(m_ref[...], w_ref[...])
    o_ref[...] = x_ref[...] + _rms_unit(y) * g_ref[...]


def _outproj(m, w, x, g, li):
    M, D = x.shape
    tm = _tile(M, 512)
    return pl.pallas_call(
        _outproj_kernel,
        grid=(M // tm,),
        in_specs=[pl.BlockSpec((tm, D), lambda i: (i, 0)),
                  pl.BlockSpec((None, D, D), lambda i: (li, 0, 0)),
                  pl.BlockSpec((tm, D), lambda i: (i, 0)),
                  pl.BlockSpec((1, D), lambda i: (0, 0))],
        out_specs=pl.BlockSpec((tm, D), lambda i: (i, 0)),
        out_shape=jax.ShapeDtypeStruct((M, D), f32),
        compiler_params=_cparams("parallel"),
        name="outproj")(m, w, x, g)


def _ffn_kernel(x_ref, g1_ref, wg_ref, wu_ref, wd_ref, g2_ref, o_ref, h_ref, acc_ref):
    j = pl.program_id(1)

    @pl.when(j == 0)
    def _():
        h_ref[...] = (_rms_unit(x_ref[...]) * g1_ref[...]).astype(bf16)
        acc_ref[...] = jnp.zeros_like(acc_ref)

    h = h_ref[...]
    a = _dot(h, wg_ref[...])
    u = _dot(h, wu_ref[...])
    act = (a * jax.nn.sigmoid(a) * u).astype(bf16)
    acc_ref[...] += _dot(act, wd_ref[...])

    @pl.when(j == pl.num_programs(1) - 1)
    def _():
        o_ref[...] = x_ref[...] + _rms_unit(acc_ref[...]) * g2_ref[...]


def _ffn(x, g1, wg, wu, wd, g2, li):
    M, D = x.shape
    F = wg.shape[2]
    tm, tf = _tile(M, 512), _tile(F, 512, LANES)
    return pl.pallas_call(
        _ffn_kernel,
        grid=(M // tm, F // tf),
        in_specs=[pl.BlockSpec((tm, D), lambda i, j: (i, 0)),
                  pl.BlockSpec((1, D), lambda i, j: (0, 0)),
                  pl.BlockSpec((None, D, tf), lambda i, j: (li, 0, j)),
                  pl.BlockSpec((None, D, tf), lambda i, j: (li, 0, j)),
                  pl.BlockSpec((None, tf, D), lambda i, j: (li, j, 0)),
                  pl.BlockSpec((1, D), lambda i, j: (0, 0))],
        out_specs=pl.BlockSpec((tm, D), lambda i, j: (i, 0)),
        out_shape=jax.ShapeDtypeStruct((M, D), f32),
        scratch_shapes=[pltpu.VMEM((tm, D), bf16), pltpu.VMEM((tm, D), f32)],
        compiler_params=_cparams("parallel", "arbitrary"),
        name="ffn")(x, g1, wg, wu, wd, g2)


def _ple_kernel(x_ref, p_ref, wi_ref, wg_ref, g_ref, o_ref):
    x = x_ref[...]
    e = _dot(p_ref[...].astype(bf16), wi_ref[...]) * jax.nn.sigmoid(
        _dot(_rms_unit(x).astype(bf16), wg_ref[...]))
    o_ref[...] = x + _rms_unit(e) * g_ref[...]


def _ple(x, p, wi, wg, g, li):
    M, D = x.shape
    P = p.shape[2]
    tm = _tile(M, 512)
    return pl.pallas_call(
        _ple_kernel,
        grid=(M // tm,),
        in_specs=[pl.BlockSpec((tm, D), lambda i: (i, 0)),
                  pl.BlockSpec((None, tm, P), lambda i: (li, i, 0)),
                  pl.BlockSpec((None, P, D), lambda i: (li, 0, 0)),
                  pl.BlockSpec((None, D, D), lambda i: (li, 0, 0)),
                  pl.BlockSpec((1, D), lambda i: (0, 0))],
        out_specs=pl.BlockSpec((tm, D), lambda i: (i, 0)),
        out_shape=jax.ShapeDtypeStruct((M, D), f32),
        compiler_params=_cparams("parallel"),
        name="ple")(x, p, wi, wg, g)


def _conv_prompt_kernel(cb_ref, cc_ref, ch_ref, ccp_ref, chp_ref, w_ref, o_ref, st_ref, *, tiles_per_seq):
    tm = cb_ref.shape[0]
    first = pl.program_id(0) % tiles_per_seq == 0
    u = cc_ref[...] * ch_ref[...]
    up = ccp_ref[...] * chp_ref[...]
    p0 = jnp.where(first, 0.0, up[6:7])
    p1 = jnp.where(first, 0.0, up[7:8])
    row = lax.broadcasted_iota(jnp.int32, u.shape, 0)
    u1 = jnp.where(row == 0, p1, pltpu.roll(u, 1, 0))
    u2 = jnp.where(row == 0, p0, jnp.where(row == 1, p1, pltpu.roll(u, 2, 0)))
    conv = w_ref[0:1] * u2 + w_ref[1:2] * u1 + w_ref[2:3] * u
    o_ref[...] = (cb_ref[...] * conv).astype(bf16)
    st_ref[0] = u[tm - (CONV_W - 1):tm]


def _conv_prompt(z, conv_w, conv_off, B, T):
    M = z.shape[0]
    DC = conv_w.shape[1]
    tm = _tile(T, 512)
    c0 = conv_off // DC
    prev = lambda i: (jnp.maximum(i * (tm // 8) - 1, 0))
    return pl.pallas_call(
        functools.partial(_conv_prompt_kernel, tiles_per_seq=T // tm),
        grid=(M // tm,),
        in_specs=[pl.BlockSpec((tm, DC), lambda i: (i, c0)),
                  pl.BlockSpec((tm, DC), lambda i: (i, c0 + 1)),
                  pl.BlockSpec((tm, DC), lambda i: (i, c0 + 2)),
                  pl.BlockSpec((8, DC), lambda i: (prev(i), c0 + 1)),
                  pl.BlockSpec((8, DC), lambda i: (prev(i), c0 + 2)),
                  pl.BlockSpec((CONV_W, DC), lambda i: (0, 0))],
        out_specs=[pl.BlockSpec((tm, DC), lambda i: (i, 0)),
                   pl.BlockSpec((1, CONV_W - 1, DC), lambda i: (i // (T // tm), 0, 0))],
        out_shape=[jax.ShapeDtypeStruct((M, DC), bf16),
                   jax.ShapeDtypeStruct((B, CONV_W - 1, DC), f32)],
        compiler_params=_cparams("arbitrary"),
        name="conv_prompt")(z, z, z, z, z, conv_w)


def _conv_sample_kernel(cb_ref, cc_ref, ch_ref, b0_ref, b1_ref, w_ref, o_ref, u_ref):
    u = cc_ref[...] * ch_ref[...]
    conv = w_ref[0:1] * b0_ref[...] + w_ref[1:2] * b1_ref[...] + w_ref[2:3] * u
    o_ref[...] = (cb_ref[...] * conv).astype(bf16)
    u_ref[...] = u


def _conv_sample(z, conv_w, conv_off, buf0, buf1):
    M = z.shape[0]
    DC = conv_w.shape[1]
    c0 = conv_off // DC
    col = lambda c: pl.BlockSpec((M, DC), lambda i: (0, c))
    return pl.pallas_call(
        _conv_sample_kernel,
        grid=(1,),
        in_specs=[col(c0), col(c0 + 1), col(c0 + 2), col(0), col(0),
                  pl.BlockSpec((CONV_W, DC), lambda i: (0, 0))],
        out_specs=[col(0), col(0)],
        out_shape=[jax.ShapeDtypeStruct((M, DC), bf16), jax.ShapeDtypeStruct((M, DC), f32)],
        compiler_params=_cparams("arbitrary"),
        name="conv_sample")(z, z, z, buf0, buf1, conv_w)


def _head_ones():
    r = lax.broadcasted_iota(jnp.int32, (2 * LANES, 2 * LANES), 0) >> 6
    c = lax.broadcasted_iota(jnp.int32, (2 * LANES, 2 * LANES), 1) >> 6
    return (r == c).astype(bf16)


def _head_sum(x, ones):
    hi = x.astype(bf16)
    lo = (x - hi.astype(f32)).astype(bf16)
    s = _dot(jnp.concatenate([hi, lo], axis=1), ones)
    return s[:, :LANES] + s[:, LANES:]


def _wkv_prep(rs, ks, vs, ls, wd0, a0, k_k, k_a, r_k, wda, wgu, ones):
    lane = lax.broadcasted_iota(jnp.int32, rs.shape, 1)
    l0 = ls[:, :LANES]
    lin = jnp.where(lane < R_DECAY, jnp.tanh(l0), l0).astype(bf16)
    da = _dot(lin, wda)
    lw = -math.exp(-0.5) * jax.nn.sigmoid(wd0 + da[:, :LANES])
    a = jax.nn.sigmoid(a0 + da[:, LANES:])
    g = _dot(jax.nn.sigmoid(ls[:, LANES:]).astype(bf16), wgu)
    kkr = ks * k_k
    kap = kkr / jnp.maximum(jnp.sqrt(_head_sum(kkr * kkr, ones)), 1e-12)
    kt = ks * (1.0 + (a - 1.0) * k_a)
    bonus = _head_sum(rs * kt * r_k, ones) * vs
    return lw, a, kap, kt, g, bonus


def _wkv_post(o, bonus, g, ln_w, ln_b, ones):
    inv = 1.0 / HEAD_DIM
    d = o - _head_sum(o, ones) * inv
    var = _head_sum(d * d, ones) * inv
    on = d * lax.rsqrt(var + GN_EPS) * ln_w + ln_b
    return ((on + bonus) * g).astype(bf16)


def _wkv_prompt_kernel(zr_ref, zk_ref, zv_ref, zl_ref, mur_ref, muk_ref, muv_ref, mul_ref,
                       wd0_ref, a0_ref, kk_ref, ka_ref, rk_ref, lnw_ref, lnb_ref, wda_ref, wgu_ref,
                       o_ref, s_ref, st_ref, pr_ref, pk_ref, pv_ref, pl_ref, o_scr):
    t = pl.program_id(1)
    nb, tb, _ = zr_ref.shape
    rows = nb * tb

    @pl.when(t == 0)
    def _():
        st_ref[...] = jnp.zeros_like(st_ref)
        pr_ref[...] = jnp.zeros_like(pr_ref)
        pk_ref[...] = jnp.zeros_like(pk_ref)
        pv_ref[...] = jnp.zeros_like(pv_ref)
        pl_ref[...] = jnp.zeros_like(pl_ref)

    def shifted(x_ref, p_ref, mu_ref):
        x = x_ref[...]
        w = x.shape[2]
        rolled = pltpu.roll(x.reshape(rows, w), 1, 0).reshape(nb, tb, w)
        row = lax.broadcasted_iota(jnp.int32, x.shape, 1)
        xp = jnp.where(row == 0, p_ref[...], rolled)
        p_ref[...] = x[:, tb - 1:tb, :]
        return (x + mu_ref[...] * (xp - x)).reshape(rows, w)

    rs = shifted(zr_ref, pr_ref, mur_ref)
    ks = shifted(zk_ref, pk_ref, muk_ref)
    vs = shifted(zv_ref, pv_ref, muv_ref)
    ls = shifted(zl_ref, pl_ref, mul_ref)
    ones = _head_ones()
    lw, a, kap, kt, g, bonus = _wkv_prep(rs, ks, vs, ls, wd0_ref[...], a0_ref[...], kk_ref[...],
                                         ka_ref[...], rk_ref[...], wda_ref[0], wgu_ref[...], ones)
    b = kap * a

    ri = lax.broadcasted_iota(jnp.int32, (tb, tb), 0)
    ci = lax.broadcasted_iota(jnp.int32, (tb, tb), 1)
    tri = (((ri >> 6) == (ci >> 6)) & (ci <= ri)).astype(bf16)
    h1 = lw.astype(bf16)
    h2 = (lw - h1.astype(f32)).astype(bf16)
    hl = jnp.concatenate([h1, h2], axis=1)
    cums = []
    for i in range(nb):
        cs = _dot(tri, hl[i * tb:(i + 1) * tb])
        cums.append(cs[:, :LANES] + cs[:, LANES:])

    rr = lax.broadcasted_iota(jnp.int32, (LANES, LANES), 0)
    cc = lax.broadcasted_iota(jnp.int32, (LANES, LANES), 1)
    same_head = (rr >> 6) == (cc >> 6)
    tt, ss = rr & (CHUNK - 1), cc & (CHUNK - 1)
    strict = same_head & (ss < tt)
    incl = same_head & (ss <= tt)
    eye = rr == cc

    def level_mask(k):
        rb, cb = tt >> k, ss >> k
        return same_head & ((rb >> 1) == (cb >> 1)) & ((rb & 1) == 1) & ((cb & 1) == 0)

    def pair(x):
        return jnp.where(same_head, jnp.concatenate([x, x], axis=0), 0.0)

    ncs = tb // CHUNK
    units = [(i, c) for i in range(nb) for c in range(ncs)]
    n = len(units)
    kh, rh, vb, bd, kd, x, y, e_last = [], [], [], [], [], [], [], []
    for i, c in units:
        sl = slice(i * tb + c * CHUNK, i * tb + (c + 1) * CHUNK)
        cm = cums[i][c * CHUNK:(c + 1) * CHUNK]
        c_last = cums[i][(c + 1) * CHUNK - 1:(c + 1) * CHUNK]
        e_p, e_m = jnp.exp(cm), jnp.exp(-cm)
        e_x, e_d = jnp.exp(cm - lw[sl]), jnp.exp(c_last - cm)
        kh.append(pair(kap[sl] * e_x).astype(bf16))
        rh.append(pair(rs[sl] * e_p))
        vb.append(pair(vs[sl]).astype(bf16))
        bd.append(pair(b[sl] * e_d).astype(bf16))
        kd.append(pair(kt[sl] * e_d).astype(bf16))
        x.append(jnp.concatenate([kh[-1], rh[-1].astype(bf16)], axis=0))
        ybc, ykc = (b[sl] * e_m).astype(bf16), (kt[sl] * e_m).astype(bf16)
        y.append(jnp.concatenate([ybc, ybc, ykc, ykc], axis=0))
        e_last.append(jnp.exp(c_last))

    each = lambda f: [f(u) for u in range(n)]
    att = each(lambda u: _dot_nt(x[u], y[u]))
    low = each(lambda u: jnp.where(strict, att[u][:LANES, :LANES], 0.0))
    lowb = each(lambda u: low[u].astype(bf16))
    akk = each(lambda u: jnp.where(strict, att[u][:LANES, LANES:], 0.0).astype(bf16))
    arb_neg = each(lambda u: jnp.where(incl, -att[u][LANES:, :LANES], 0.0).astype(bf16))
    ark = each(lambda u: jnp.where(incl, att[u][LANES:, LANES:], 0.0).astype(bf16))

    m0 = level_mask(0)
    inv = each(lambda u: jnp.where(eye, 1.0, 0.0) - jnp.where(m0, low[u], 0.0))
    for k in range(1, 6):
        mk = level_mask(k)
        ib = each(lambda u: inv[u].astype(bf16))
        t1 = each(lambda u: _dot(ib[u], jnp.where(mk, lowb[u], 0.0)).astype(bf16))
        inv = each(lambda u: inv[u] - _dot(t1[u], ib[u]))
    w = each(lambda u: inv[u].astype(bf16))

    p = each(lambda u: _dot(akk[u], vb[u]).astype(bf16))
    uk = each(lambda u: _dot(w[u], jnp.concatenate([p[u], kh[u]], axis=1)).astype(bf16))
    zero = jnp.zeros((LANES, LANES), bf16)
    q = each(lambda u: _dot(jnp.concatenate([ark[u], arb_neg[u]], axis=1),
                            jnp.concatenate([jnp.concatenate([vb[u], zero], axis=1), uk[u]], axis=0)))
    o0 = each(lambda u: q[u][:, :LANES])
    rbar = each(lambda u: (rh[u] + q[u][:, LANES:]).astype(bf16))
    g0t = each(lambda u: _dot_tn(jnp.concatenate([vb[u], -uk[u][:, :LANES]], axis=0),
                                 jnp.concatenate([kd[u], bd[u]], axis=0)))
    phit = each(lambda u: (jnp.where(eye, e_last[u], 0.0) - _dot_tn(uk[u][:, LANES:], bd[u])).astype(bf16))

    for i in range(nb):
        s = st_ref[i]
        for c in range(ncs):
            u = i * ncs + c
            sb = s.astype(bf16)
            o = _dot_nt(rbar[u], sb) + o0[u]
            s = _dot(sb, phit[u]) + g0t[u]
            o_scr[i * tb + c * CHUNK:i * tb + (c + 1) * CHUNK, :] = o[:CHUNK] + o[CHUNK:]
        st_ref[i] = s

    oa = _wkv_post(o_scr[...], bonus, g, lnw_ref[...], lnb_ref[...], ones)
    o_ref[...] = oa.reshape(nb, tb, LANES)

    @pl.when(t == pl.num_programs(1) - 1)
    def _():
        s_ref[:, 0] = st_ref[...]


def _wkv_prompt(z, lp, B, T):
    DR = lp["ln_x_w"].shape[1]
    nhp = DR // LANES
    tb = _tile(T, 4 * CHUNK, CHUNK)
    lora_blk = 3 * DR // (2 * LANES)
    z3 = z.reshape(B, T, -1)
    zcol = lambda off: pl.BlockSpec((B, tb, LANES), lambda hp, t: (0, t, off + hp))
    vec = lambda off: pl.BlockSpec((1, LANES), lambda hp, t: (0, off + hp))
    in_specs = [zcol(0), zcol(nhp), zcol(2 * nhp),
                pl.BlockSpec((B, tb, 2 * LANES), lambda hp, t: (0, t, lora_blk)),
                vec(0), vec(nhp), vec(2 * nhp),
                pl.BlockSpec((1, 2 * LANES), lambda hp, t: (0, lora_blk)),
                vec(0), vec(0), vec(0), vec(0), vec(0), vec(0), vec(0),
                pl.BlockSpec((1, LANES, 2 * LANES), lambda hp, t: (hp, 0, 0)),
                pl.BlockSpec((R_GATE, LANES), lambda hp, t: (0, hp))]
    oa, s_pair = pl.pallas_call(
        _wkv_prompt_kernel,
        grid=(nhp, T // tb),
        in_specs=in_specs,
        out_specs=[pl.BlockSpec((B, tb, LANES), lambda hp, t: (0, t, hp)),
                   pl.BlockSpec((B, 1, LANES, LANES), lambda hp, t: (0, hp, 0, 0))],
        out_shape=[jax.ShapeDtypeStruct((B, T, DR), bf16),
                   jax.ShapeDtypeStruct((B, nhp, LANES, LANES), f32)],
        scratch_shapes=[pltpu.VMEM((B, LANES, LANES), f32),
                        pltpu.VMEM((B, 1, LANES), f32), pltpu.VMEM((B, 1, LANES), f32),
                        pltpu.VMEM((B, 1, LANES), f32), pltpu.VMEM((B, 1, 2 * LANES), f32),
                        pltpu.VMEM((B * tb, LANES), f32)],
        compiler_params=_cparams("parallel", "arbitrary"),
        name="wkv_prompt")(z3, z3, z3, z3, lp["mu"], lp["mu"], lp["mu"], lp["mu"],
                           lp["w_decay0"], lp["a0"], lp["k_k"], lp["k_a"], lp["r_k"],
                           lp["ln_x_w"], lp["ln_x_b"], lp["wda"], lp["wgu"])
    return oa.reshape(B * T, DR), s_pair


def _wkv_sample_prep_kernel(zr_ref, zk_ref, zv_ref, zl_ref, sr_ref, sk_ref, sv_ref, sl_ref,
                            mur_ref, muk_ref, muv_ref, mul_ref,
                            wd0_ref, a0_ref, kk_ref, ka_ref, rk_ref, wda_ref, wgu_ref,
                            r_out, w_out, kt_out, v_out, kap_out, b_out, g_out, bonus_out):
    def shifted(x_ref, p_ref, mu_ref):
        x = x_ref[...]
        return x + mu_ref[...] * (p_ref[...] - x)

    rs = shifted(zr_ref, sr_ref, mur_ref)
    ks = shifted(zk_ref, sk_ref, muk_ref)
    vs = shifted(zv_ref, sv_ref, muv_ref)
    ls = shifted(zl_ref, sl_ref, mul_ref)
    ones = _head_ones()
    lw, a, kap, kt, g, bonus = _wkv_prep(rs, ks, vs, ls, wd0_ref[...], a0_ref[...], kk_ref[...],
                                         ka_ref[...], rk_ref[...], wda_ref[0], wgu_ref[...], ones)
    r_out[...] = rs
    w_out[...] = jnp.exp(lw)
    kt_out[...] = kt
    v_out[...] = vs
    kap_out[...] = kap
    b_out[...] = kap * a
    g_out[...] = g
    bonus_out[...] = bonus


def _wkv_sample_prep(z, shift_prev, lp):
    M = z.shape[0]
    DR = lp["ln_x_w"].shape[1]
    nhp = DR // LANES
    lora_blk = 3 * DR // (2 * LANES)
    col = lambda off: pl.BlockSpec((M, LANES), lambda hp: (0, off + hp))
    vec = lambda off: pl.BlockSpec((1, LANES), lambda hp: (0, off + hp))
    wide = lambda rows: pl.BlockSpec((rows, 2 * LANES), lambda hp: (0, lora_blk))
    in_specs = [col(0), col(nhp), col(2 * nhp), wide(M),
                col(0), col(nhp), col(2 * nhp), wide(M),
                vec(0), vec(nhp), vec(2 * nhp), wide(1),
                vec(0), vec(0), vec(0), vec(0), vec(0),
                pl.BlockSpec((1, LANES, 2 * LANES), lambda hp: (hp, 0, 0)),
                pl.BlockSpec((R_GATE, LANES), lambda hp: (0, hp))]
    return pl.pallas_call(
        _wkv_sample_prep_kernel,
        grid=(nhp,),
        in_specs=in_specs,
        out_specs=[col(0)] * 8,
        out_shape=[jax.ShapeDtypeStruct((M, DR), f32)] * 8,
        compiler_params=_cparams("parallel"),
        name="wkv_sample_prep")(z, z, z, z, shift_prev, shift_prev, shift_prev, shift_prev,
                                lp["mu"], lp["mu"], lp["mu"], lp["mu"],
                                lp["w_decay0"], lp["a0"], lp["k_k"], lp["k_a"], lp["r_k"],
                                lp["wda"], lp["wgu"])


def _wkv_step_kernel(s_ref, w_ref, kap_ref, b_ref, kt_ref, r_ref, v_ref, so_ref, o_ref):
    bt, nh, _, _ = s_ref.shape
    s = s_ref[...].reshape(bt * nh, HEAD_DIM, HEAD_DIM)
    shape = s.shape
    eye = lax.broadcasted_iota(jnp.int32, shape, 1) == lax.broadcasted_iota(jnp.int32, shape, 2)
    sa = jnp.sum(s * kap_ref[...], axis=2, keepdims=True)
    v_col = jnp.sum(jnp.where(eye, v_ref[...], 0.0), axis=2, keepdims=True)
    s_new = s * w_ref[...] - sa * b_ref[...] + v_col * kt_ref[...]
    so_ref[...] = s_new.reshape(bt, nh, HEAD_DIM, HEAD_DIM)
    o_col = jnp.sum(s_new * r_ref[...], axis=2, keepdims=True)
    o_ref[...] = jnp.sum(jnp.where(eye, o_col, 0.0), axis=1, keepdims=True)


def _wkv_step(s, w, kap, b, kt, r, v, li):
    _, DB, H, _, _ = s.shape
    bt = _tile(DB, 8)
    mat = pl.BlockSpec((bt, H, HEAD_DIM, HEAD_DIM), lambda i: (i, 0, 0, 0))
    row = pl.BlockSpec((bt * H, 1, HEAD_DIM), lambda i: (i, 0, 0))
    return pl.pallas_call(
        _wkv_step_kernel,
        grid=(DB // bt,),
        in_specs=[pl.BlockSpec((None, bt, H, HEAD_DIM, HEAD_DIM), lambda i: (li, i, 0, 0, 0))] + [row] * 6,
        out_specs=[mat, row],
        out_shape=[jax.ShapeDtypeStruct(s.shape[1:], f32), jax.ShapeDtypeStruct((DB * H, 1, HEAD_DIM), f32)],
        compiler_params=_cparams("parallel"),
        name="wkv_step")(s, w, kap, b, kt, r, v)


def _wkv_sample_post_kernel(o_ref, bonus_ref, g_ref, lnw_ref, lnb_ref, oa_ref):
    oa_ref[...] = _wkv_post(o_ref[...], bonus_ref[...], g_ref[...], lnw_ref[...], lnb_ref[...], _head_ones())


def _wkv_sample_post(o, bonus, g, lp):
    M, DR = o.shape
    col = pl.BlockSpec((M, LANES), lambda hp: (0, hp))
    vec = pl.BlockSpec((1, LANES), lambda hp: (0, hp))
    return pl.pallas_call(
        _wkv_sample_post_kernel,
        grid=(DR // LANES,),
        in_specs=[col, col, col, vec, vec],
        out_specs=col,
        out_shape=jax.ShapeDtypeStruct((M, DR), bf16),
        compiler_params=_cparams("parallel"),
        name="wkv_sample_post")(o, bonus, g, lp["ln_x_w"], lp["ln_x_b"])


def _layer_params(i, DR, g_mix_pre, g_mix_post, g_ffn_pre, g_ffn_post, g_ple_post,
                  mu_shift, w_decay0, w_decay_up, a0, w_a_up, w_gate_up, k_k, k_a, r_k,
                  ln_x_w, ln_x_b, conv_w):
    nhp = DR // LANES
    row = lambda v: v[i].reshape(1, -1)
    wdu = w_decay_up[i].reshape(R_DECAY, nhp, LANES).transpose(1, 0, 2)
    wau = w_a_up[i].reshape(R_AAA, nhp, LANES).transpose(1, 0, 2)
    wda = jnp.concatenate([jnp.concatenate([wdu, jnp.zeros_like(wdu)], axis=2),
                           jnp.concatenate([jnp.zeros_like(wau), wau], axis=2)], axis=1).astype(bf16)
    return dict(
        g_mix_pre=row(g_mix_pre), g_mix_post=row(g_mix_post), g_ffn_pre=row(g_ffn_pre),
        g_ffn_post=row(g_ffn_post), g_ple_post=row(g_ple_post),
        mu=row(mu_shift), w_decay0=row(w_decay0), a0=row(a0), k_k=row(k_k), k_a=row(k_a),
        r_k=row(r_k), ln_x_w=row(ln_x_w), ln_x_b=row(ln_x_b), wda=wda, wgu=w_gate_up[i].astype(bf16),
        conv_w=conv_w[i])


def _mix_tail(x, p, z, oa, ci, lp, wts, gate_off, li):
    merged = _merge(oa, ci, wts["w_branch_a"], wts["w_branch_c"], z, gate_off, li)
    x = _outproj(merged, wts["w_out"], x, lp["g_mix_post"], li)
    x = _ffn(x, lp["g_ffn_pre"], wts["w_ffn_gate"], wts["w_ffn_up"], wts["w_ffn_down"], lp["g_ffn_post"], li)
    return _ple(x, p, wts["w_ple_in"], wts["w_ple_gate"], lp["g_ple_post"], li)


def kernel(x_prompt, x_sample, state_wkv, state_shift, state_conv, p_prompt, p_sample, g_mix_pre, g_mix_post, g_ffn_pre, g_ffn_post, g_ple_post, w_in, mu_shift, w_decay0, w_decay_up, a0, w_a_up, w_gate_up, k_k, k_a, r_k, ln_x_w, ln_x_b, conv_w, w_branch_a, w_branch_c, w_out, w_ffn_gate, w_ffn_up, w_ffn_down, w_ple_in, w_ple_gate):
    B, T, D = x_prompt.shape
    DB, DT, _ = x_sample.shape
    depth = w_in.shape[0]
    DR = w_decay0.shape[1]
    DC = conv_w.shape[2]
    H = DR // HEAD_DIM
    d_shift = mu_shift.shape[1]
    assert DT == 1 and T % CHUNK == 0 and T >= CONV_W - 1
    assert DR % (2 * LANES) == 0 and DC == DR and d_shift == 3 * DR + R_DECAY + R_AAA + R_GATE
    d_shift_p = -(-d_shift // DC) * DC
    conv_off = d_shift_p
    gate_off = d_shift_p + 3 * DC

    wts = dict(
        w_in=jnp.concatenate([w_in[:, :, :d_shift], jnp.zeros((depth, D, d_shift_p - d_shift), f32),
                              w_in[:, :, d_shift:]], axis=2).astype(bf16),
        w_branch_a=w_branch_a.astype(bf16), w_branch_c=w_branch_c.astype(bf16), w_out=w_out.astype(bf16),
        w_ffn_gate=w_ffn_gate.astype(bf16), w_ffn_up=w_ffn_up.astype(bf16),
        w_ffn_down=w_ffn_down.astype(bf16), w_ple_in=w_ple_in.astype(bf16),
        w_ple_gate=w_ple_gate.astype(bf16))
    pp = p_prompt.reshape(depth, B * T, -1)
    ps = p_sample.reshape(depth, DB, -1)

    xp = x_prompt.reshape(B * T, D)
    xs = x_sample.reshape(DB, D)
    outs = [[] for _ in range(6)]
    for i in range(depth):
        lp = _layer_params(i, DR, g_mix_pre, g_mix_post, g_ffn_pre, g_ffn_post, g_ple_post,
                           mu_shift, w_decay0, w_decay_up, a0, w_a_up, w_gate_up, k_k, k_a,
                           r_k.reshape(depth, DR), ln_x_w, ln_x_b, conv_w)

        zp = _inproj(xp, lp["g_mix_pre"], wts["w_in"], i)
        oa, s_pair = _wkv_prompt(zp, lp, B, T)
        ci, conv_p = _conv_prompt(zp, lp["conv_w"], conv_off, B, T)
        xp = _mix_tail(xp, pp, zp, oa, ci, lp, wts, gate_off, i)
        s_p = jnp.stack([s_pair[:, :, :HEAD_DIM, :HEAD_DIM], s_pair[:, :, HEAD_DIM:, HEAD_DIM:]], axis=2)
        outs[0].append(s_p.reshape(B, H, HEAD_DIM, HEAD_DIM))
        outs[1].append(zp.reshape(B, T, -1)[:, T - 1, :d_shift])
        outs[2].append(conv_p)

        zs = _inproj(xs, lp["g_mix_pre"], wts["w_in"], i)
        r, w, kt, v, kap, b, g, bonus = _wkv_sample_prep(zs, state_shift[i], lp)
        rows = lambda a_: a_.reshape(DB * H, 1, HEAD_DIM)
        s_new, o = _wkv_step(state_wkv, rows(w), rows(kap), rows(b), rows(kt), rows(r), rows(v), i)
        oa = _wkv_sample_post(o.reshape(DB, DR), bonus, g, lp)
        ci, u = _conv_sample(zs, lp["conv_w"], conv_off, state_conv[i][:, 0], state_conv[i][:, 1])
        xs = _mix_tail(xs, ps, zs, oa, ci, lp, wts, gate_off, i)
        outs[3].append(s_new)
        outs[4].append(zs[:, :d_shift])
        outs[5].append(jnp.stack([state_conv[i][:, 1], u], axis=1))

    return (xp.reshape(B, T, D), xs.reshape(DB, 1, D)) + tuple(jnp.stack(o) for o in outs)
```

```python
import functools
import math

import jax
import jax.numpy as jnp
from jax import lax
from jax.experimental import pallas as pl
from jax.experimental.pallas import tpu as pltpu

HEAD_DIM = 64
LANES = 128
CHUNK = 64
R_DECAY = 64
R_AAA = 64
R_GATE = 128
CONV_W = 3
RMS_EPS = 1e-6
GN_EPS = 64e-5
VMEM_LIMIT_MB = 56

f32 = jnp.float32
bf16 = jnp.bfloat16


def _cparams(*sem):
    return pltpu.CompilerParams(dimension_semantics=sem, vmem_limit_bytes=VMEM_LIMIT_MB * 2**20)


def _tile(n, pref, mult=8):
    if n <= pref:
        return n
    for t in range(pref - pref % mult, 0, -mult):
        if n % t == 0:
            return t
    return n


def _dot(a, b):
    return jnp.dot(a, b, preferred_element_type=f32)


def _dot_nt(a, b):
    return lax.dot_general(a, b, (((1,), (1,)), ((), ())), preferred_element_type=f32)


def _dot_tn(a, b):
    return lax.dot_general(a, b, (((0,), (0,)), ((), ())), preferred_element_type=f32)


def _rms_unit(x):
    return x * lax.rsqrt(jnp.mean(x * x, axis=-1, keepdims=True) + RMS_EPS)


def _inproj_kernel(x_ref, g_ref, w_ref, o_ref, h_ref):
    @pl.when(pl.program_id(1) == 0)
    def _():
        h_ref[...] = (_rms_unit(x_ref[...]) * g_ref[...]).astype(bf16)
    o_ref[...] = _dot(h_ref[...], w_ref[0])


def _inproj(x, g, w, li, col0, NZ):
    M, D = x.shape
    assert col0 % LANES == 0
    tm, tn = _tile(M, 1024), _tile(NZ, 1792, LANES)
    return pl.pallas_call(
        _inproj_kernel,
        grid=(M // tm, NZ // tn),
        in_specs=[pl.BlockSpec((tm, D), lambda i, j: (i, 0)),
                  pl.BlockSpec((1, D), lambda i, j: (0, 0)),
                  pl.BlockSpec((pl.Element(1), pl.Element(D), pl.Element(tn)),
                               lambda i, j: (li, 0, pl.multiple_of(col0 + j * tn, LANES)))],
        out_specs=pl.BlockSpec((tm, tn), lambda i, j: (i, j)),
        out_shape=jax.ShapeDtypeStruct((M, NZ), f32),
        scratch_shapes=[pltpu.VMEM((tm, D), bf16)],
        compiler_params=_cparams("parallel", "arbitrary"),
        name="inproj")(x, g, w)


def _mixout_kernel(oa_ref, ci_ref, wa_ref, wc_ref, zg_ref, wo_ref, x_ref, g_ref, o_ref):
    d = x_ref.shape[1]
    ya = _dot(oa_ref[...], wa_ref[...])
    yc = _dot(ci_ref[...], wc_ref[...])
    merged = jax.nn.sigmoid(zg_ref[:, :d]) * ya + jax.nn.sigmoid(zg_ref[:, d:]) * yc
    y = _dot(merged.astype(bf16), wo_ref[...])
    o_ref[...] = x_ref[...] + _rms_unit(y) * g_ref[...]


def _mixout(oa, ci, wa, wc, zg, wo, x, g, li):
    M, D = x.shape
    DC = oa.shape[1]
    tm = _tile(M, 256)
    resident = lambda shape: pl.BlockSpec((None,) + shape, lambda i: (li, 0, 0), pipeline_mode=pl.Buffered(1))
    return pl.pallas_call(
        _mixout_kernel,
        grid=(M // tm,),
        in_specs=[pl.BlockSpec((tm, DC), lambda i: (i, 0)),
                  pl.BlockSpec((tm, DC), lambda i: (i, 0)),
                  resident((DC, D)), resident((DC, D)),
                  pl.BlockSpec((tm, 2 * D), lambda i: (i, 0)),
                  resident((D, D)),
                  pl.BlockSpec((tm, D), lambda i: (i, 0)),
                  pl.BlockSpec((1, D), lambda i: (0, 0))],
        out_specs=pl.BlockSpec((tm, D), lambda i: (i, 0)),
        out_shape=jax.ShapeDtypeStruct((M, D), f32),
        compiler_params=_cparams("parallel"),
        name="mixout")(oa, ci, wa, wc, zg, wo, x, g)


def _ffn_kernel(x_ref, g1_ref, wg_ref, wu_ref, wd_ref, g2_ref, o_ref, h_ref):
    j = pl.program_id(1)

    @pl.when(j == 0)
    def _():
        h_ref[...] = (_rms_unit(x_ref[...]) * g1_ref[...]).astype(bf16)
        o_ref[...] = jnp.zeros_like(o_ref)

    h = h_ref[...]
    a = _dot(h, wg_ref[...])
    u = _dot(h, wu_ref[...])
    act = (a * jax.nn.sigmoid(a) * u).astype(bf16)
    o_ref[...] += _dot(act, wd_ref[...])

    @pl.when(j == pl.num_programs(1) - 1)
    def _():
        o_ref[...] = x_ref[...] + _rms_unit(o_ref[...]) * g2_ref[...]


def _ffn(x, g1, wg, wu, wd, g2, li):
    M, D = x.shape
    F = wg.shape[2]
    tm, tf = _tile(M, 512), _tile(F, 512, LANES)
    return pl.pallas_call(
        _ffn_kernel,
        grid=(M // tm, F // tf),
        in_specs=[pl.BlockSpec((tm, D), lambda i, j: (i, 0)),
                  pl.BlockSpec((1, D), lambda i, j: (0, 0)),
                  pl.BlockSpec((None, D, tf), lambda i, j: (li, 0, j)),
                  pl.BlockSpec((None, D, tf), lambda i, j: (li, 0, j)),
                  pl.BlockSpec((None, tf, D), lambda i, j: (li, j, 0)),
                  pl.BlockSpec((1, D), lambda i, j: (0, 0))],
        out_specs=pl.BlockSpec((tm, D), lambda i, j: (i, 0)),
        out_shape=jax.ShapeDtypeStruct((M, D), f32),
        scratch_shapes=[pltpu.VMEM((tm, D), bf16)],
        compiler_params=_cparams("parallel", "arbitrary"),
        name="ffn")(x, g1, wg, wu, wd, g2)


def _ple_kernel(x_ref, p_ref, wi_ref, wg_ref, g_ref, o_ref):
    x = x_ref[...]
    e = _dot(p_ref[...].astype(bf16), wi_ref[...]) * jax.nn.sigmoid(
        _dot(_rms_unit(x).astype(bf16), wg_ref[...]))
    o_ref[...] = x + _rms_unit(e) * g_ref[...]


def _ple(x, p, wi, wg, g, li):
    M, D = x.shape
    P = p.shape[2]
    tm = _tile(M, 512)
    return pl.pallas_call(
        _ple_kernel,
        grid=(M // tm,),
        in_specs=[pl.BlockSpec((tm, D), lambda i: (i, 0)),
                  pl.BlockSpec((None, tm, P), lambda i: (li, i, 0)),
                  pl.BlockSpec((None, P, D), lambda i: (li, 0, 0)),
                  pl.BlockSpec((None, D, D), lambda i: (li, 0, 0)),
                  pl.BlockSpec((1, D), lambda i: (0, 0))],
        out_specs=pl.BlockSpec((tm, D), lambda i: (i, 0)),
        out_shape=jax.ShapeDtypeStruct((M, D), f32),
        compiler_params=_cparams("parallel"),
        name="ple")(x, p, wi, wg, g)


def _conv_prompt_kernel(cb_ref, cc_ref, ch_ref, ccp_ref, chp_ref, w_ref, o_ref, st_ref, *, tiles_per_seq):
    tm = cb_ref.shape[0]
    first = pl.program_id(0) % tiles_per_seq == 0
    u = cc_ref[...] * ch_ref[...]
    up = ccp_ref[...] * chp_ref[...]
    p0 = jnp.where(first, 0.0, up[6:7])
    p1 = jnp.where(first, 0.0, up[7:8])
    row = lax.broadcasted_iota(jnp.int32, u.shape, 0)
    u1 = jnp.where(row == 0, p1, pltpu.roll(u, 1, 0))
    u2 = jnp.where(row == 0, p0, jnp.where(row == 1, p1, pltpu.roll(u, 2, 0)))
    conv = w_ref[0:1] * u2 + w_ref[1:2] * u1 + w_ref[2:3] * u
    o_ref[...] = (cb_ref[...] * conv).astype(bf16)
    st_ref[0] = u[tm - (CONV_W - 1):tm]


def _conv_prompt(z, conv_w, B, T):
    M = z.shape[0]
    DC = conv_w.shape[1]
    tm = _tile(T, 512)
    prev = lambda i: (jnp.maximum(i * (tm // 8) - 1, 0))
    return pl.pallas_call(
        functools.partial(_conv_prompt_kernel, tiles_per_seq=T // tm),
        grid=(M // tm,),
        in_specs=[pl.BlockSpec((tm, DC), lambda i: (i, 0)),
                  pl.BlockSpec((tm, DC), lambda i: (i, 1)),
                  pl.BlockSpec((tm, DC), lambda i: (i, 2)),
                  pl.BlockSpec((8, DC), lambda i: (prev(i), 1)),
                  pl.BlockSpec((8, DC), lambda i: (prev(i), 2)),
                  pl.BlockSpec((CONV_W, DC), lambda i: (0, 0))],
        out_specs=[pl.BlockSpec((tm, DC), lambda i: (i, 0)),
                   pl.BlockSpec((1, CONV_W - 1, DC), lambda i: (i // (T // tm), 0, 0))],
        out_shape=[jax.ShapeDtypeStruct((M, DC), bf16),
                   jax.ShapeDtypeStruct((B, CONV_W - 1, DC), f32)],
        compiler_params=_cparams("arbitrary"),
        name="conv_prompt")(z, z, z, z, z, conv_w)


def _conv_sample_kernel(cb_ref, cc_ref, ch_ref, b0_ref, b1_ref, w_ref, o_ref, u_ref):
    u = cc_ref[...] * ch_ref[...]
    conv = w_ref[0:1] * b0_ref[...] + w_ref[1:2] * b1_ref[...] + w_ref[2:3] * u
    o_ref[...] = (cb_ref[...] * conv).astype(bf16)
    u_ref[...] = u


def _conv_sample(z, conv_w, buf0, buf1):
    M = z.shape[0]
    DC = conv_w.shape[1]
    col = lambda c: pl.BlockSpec((M, DC), lambda i: (0, c))
    return pl.pallas_call(
        _conv_sample_kernel,
        grid=(1,),
        in_specs=[col(0), col(1), col(2), col(0), col(0),
                  pl.BlockSpec((CONV_W, DC), lambda i: (0, 0))],
        out_specs=[col(0), col(0)],
        out_shape=[jax.ShapeDtypeStruct((M, DC), bf16), jax.ShapeDtypeStruct((M, DC), f32)],
        compiler_params=_cparams("arbitrary"),
        name="conv_sample")(z, z, z, buf0, buf1, conv_w)


def _head_ones():
    r = lax.broadcasted_iota(jnp.int32, (2 * LANES, 2 * LANES), 0) >> 6
    c = lax.broadcasted_iota(jnp.int32, (2 * LANES, 2 * LANES), 1) >> 6
    return (r == c).astype(bf16)


def _head_sum(x, ones):
    hi = x.astype(bf16)
    lo = (x - hi.astype(f32)).astype(bf16)
    s = _dot(jnp.concatenate([hi, lo], axis=1), ones)
    return s[:, :LANES] + s[:, LANES:]


def _wkv_prep(rs, ks, vs, ls, wd0, a0, k_k, k_a, r_k, wda, wgu, ones):
    lane = lax.broadcasted_iota(jnp.int32, rs.shape, 1)
    l0 = ls[:, :LANES]
    lin = jnp.where(lane < R_DECAY, jnp.tanh(l0), l0).astype(bf16)
    da = _dot(lin, wda)
    lw = -math.exp(-0.5) * jax.nn.sigmoid(wd0 + da[:, :LANES])
    a = jax.nn.sigmoid(a0 + da[:, LANES:])
    g = _dot(jax.nn.sigmoid(ls[:, LANES:]).astype(bf16), wgu)
    kkr = ks * k_k
    kap = kkr / jnp.maximum(jnp.sqrt(_head_sum(kkr * kkr, ones)), 1e-12)
    kt = ks * (1.0 + (a - 1.0) * k_a)
    bonus = _head_sum(rs * kt * r_k, ones) * vs
    return lw, a, kap, kt, g, bonus


def _wkv_post(o, bonus, g, ln_w, ln_b, ones):
    inv = 1.0 / HEAD_DIM
    d = o - _head_sum(o, ones) * inv
    var = _head_sum(d * d, ones) * inv
    on = d * lax.rsqrt(var + GN_EPS) * ln_w + ln_b
    return ((on + bonus) * g).astype(bf16)


def _wkv_prompt_kernel(zr_ref, zk_ref, zv_ref, zl_ref, mur_ref, muk_ref, muv_ref, mul_ref,
                       wd0_ref, a0_ref, kk_ref, ka_ref, rk_ref, lnw_ref, lnb_ref, wda_ref, wgu_ref,
                       o_ref, s_ref, st_ref, pr_ref, pk_ref, pv_ref, pl_ref, o_scr):
    t = pl.program_id(1)
    nb, tb, _ = zr_ref.shape
    rows = nb * tb

    @pl.when(t == 0)
    def _():
        st_ref[...] = jnp.zeros_like(st_ref)
        pr_ref[...] = jnp.zeros_like(pr_ref)
        pk_ref[...] = jnp.zeros_like(pk_ref)
        pv_ref[...] = jnp.zeros_like(pv_ref)
        pl_ref[...] = jnp.zeros_like(pl_ref)

    def shifted(x_ref, p_ref, mu_ref):
        x = x_ref[...]
        w = x.shape[2]
        rolled = pltpu.roll(x.reshape(rows, w), 1, 0).reshape(nb, tb, w)
        row = lax.broadcasted_iota(jnp.int32, x.shape, 1)
        xp = jnp.where(row == 0, p_ref[...], rolled)
        p_ref[...] = x[:, tb - 1:tb, :]
        return (x + mu_ref[...] * (xp - x)).reshape(rows, w)

    rs = shifted(zr_ref, pr_ref, mur_ref)
    ks = shifted(zk_ref, pk_ref, muk_ref)
    vs = shifted(zv_ref, pv_ref, muv_ref)
    ls = shifted(zl_ref, pl_ref, mul_ref)
    ones = _head_ones()
    lw, a, kap, kt, g, bonus = _wkv_prep(rs, ks, vs, ls, wd0_ref[...], a0_ref[...], kk_ref[...],
                                         ka_ref[...], rk_ref[...], wda_ref[0], wgu_ref[...], ones)
    b = kap * a

    ri = lax.broadcasted_iota(jnp.int32, (tb, tb), 0)
    ci = lax.broadcasted_iota(jnp.int32, (tb, tb), 1)
    tri = (((ri >> 6) == (ci >> 6)) & (ci <= ri)).astype(bf16)
    h1 = lw.astype(bf16)
    h2 = (lw - h1.astype(f32)).astype(bf16)
    hl = jnp.concatenate([h1, h2], axis=1)
    cums = []
    for i in range(nb):
        cs = _dot(tri, hl[i * tb:(i + 1) * tb])
        cums.append(cs[:, :LANES] + cs[:, LANES:])

    rr = lax.broadcasted_iota(jnp.int32, (LANES, LANES), 0)
    cc = lax.broadcasted_iota(jnp.int32, (LANES, LANES), 1)
    same_head = (rr >> 6) == (cc >> 6)
    tt, ss = rr & (CHUNK - 1), cc & (CHUNK - 1)
    strict = same_head & (ss < tt)
    incl = same_head & (ss <= tt)
    eye = rr == cc

    def level_mask(k):
        rb, cb = tt >> k, ss >> k
        return same_head & ((rb >> 1) == (cb >> 1)) & ((rb & 1) == 1) & ((cb & 1) == 0)

    def pair(x):
        return jnp.where(same_head, jnp.concatenate([x, x], axis=0), 0.0)

    ncs = tb // CHUNK
    units = [(i, c) for i in range(nb) for c in range(ncs)]
    n = len(units)
    kh, rh, vb, bd, kd, x, y, e_last = [], [], [], [], [], [], [], []
    for i, c in units:
        sl = slice(i * tb + c * CHUNK, i * tb + (c + 1) * CHUNK)
        cm = cums[i][c * CHUNK:(c + 1) * CHUNK]
        c_last = cums[i][(c + 1) * CHUNK - 1:(c + 1) * CHUNK]
        e_p, e_m = jnp.exp(cm), jnp.exp(-cm)
        e_x, e_d = jnp.exp(cm - lw[sl]), jnp.exp(c_last - cm)
        kh.append(pair(kap[sl] * e_x).astype(bf16))
        rh.append(pair(rs[sl] * e_p))
        vb.append(pair(vs[sl]).astype(bf16))
        bd.append(pair(b[sl] * e_d).astype(bf16))
        kd.append(pair(kt[sl] * e_d).astype(bf16))
        x.append(jnp.concatenate([kh[-1], rh[-1].astype(bf16)], axis=0))
        ybc, ykc = (b[sl] * e_m).astype(bf16), (kt[sl] * e_m).astype(bf16)
        y.append(jnp.concatenate([ybc, ybc, ykc, ykc], axis=0))
        e_last.append(jnp.exp(c_last))

    each = lambda f: [f(u) for u in range(n)]
    att = each(lambda u: _dot_nt(x[u], y[u]))
    low = each(lambda u: jnp.where(strict, att[u][:LANES, :LANES], 0.0))
    lowb = each(lambda u: low[u].astype(bf16))
    akk = each(lambda u: jnp.where(strict, att[u][:LANES, LANES:], 0.0).astype(bf16))
    arb_neg = each(lambda u: jnp.where(incl, -att[u][LANES:, :LANES], 0.0).astype(bf16))
    ark = each(lambda u: jnp.where(incl, att[u][LANES:, LANES:], 0.0).astype(bf16))

    m0 = level_mask(0)
    inv = each(lambda u: jnp.where(eye, 1.0, 0.0) - jnp.where(m0, low[u], 0.0))
    for k in range(1, 6):
        mk = level_mask(k)
        ib = each(lambda u: inv[u].astype(bf16))
        t1 = each(lambda u: _dot(ib[u], jnp.where(mk, lowb[u], 0.0)).astype(bf16))
        inv = each(lambda u: inv[u] - _dot(t1[u], ib[u]))
    w = each(lambda u: inv[u].astype(bf16))

    p = each(lambda u: _dot(akk[u], vb[u]).astype(bf16))
    uk = each(lambda u: _dot(w[u], jnp.concatenate([p[u], kh[u]], axis=1)).astype(bf16))
    zero = jnp.zeros((LANES, LANES), bf16)
    q = each(lambda u: _dot(jnp.concatenate([ark[u], arb_neg[u]], axis=1),
                            jnp.concatenate([jnp.concatenate([vb[u], zero], axis=1), uk[u]], axis=0)))
    o0 = each(lambda u: q[u][:, :LANES])
    rbar = each(lambda u: (rh[u] + q[u][:, LANES:]).astype(bf16))
    g0t = each(lambda u: _dot_tn(jnp.concatenate([vb[u], -uk[u][:, :LANES]], axis=0),
                                 jnp.concatenate([kd[u], bd[u]], axis=0)))
    phit = each(lambda u: (jnp.where(eye, e_last[u], 0.0) - _dot_tn(uk[u][:, LANES:], bd[u])).astype(bf16))

    for i in range(nb):
        s = st_ref[i]
        for c in range(ncs):
            u = i * ncs + c
            sb = s.astype(bf16)
            o = _dot_nt(rbar[u], sb) + o0[u]
            s = _dot(sb, phit[u]) + g0t[u]
            o_scr[i * tb + c * CHUNK:i * tb + (c + 1) * CHUNK, :] = o[:CHUNK] + o[CHUNK:]
        st_ref[i] = s

    oa = _wkv_post(o_scr[...], bonus, g, lnw_ref[...], lnb_ref[...], ones)
    o_ref[...] = oa.reshape(nb, tb, LANES)

    @pl.when(t == pl.num_programs(1) - 1)
    def _():
        s_ref[:, 0] = st_ref[...]


def _wkv_prompt(z, lp, B, T):
    DR = lp["ln_x_w"].shape[1]
    nhp = DR // LANES
    tb = _tile(T, 4 * CHUNK, CHUNK)
    lora_blk = 3 * DR // (2 * LANES)
    z3 = z.reshape(B, T, -1)
    zcol = lambda off: pl.BlockSpec((B, tb, LANES), lambda hp, t: (0, t, off + hp))
    vec = lambda off: pl.BlockSpec((1, LANES), lambda hp, t: (0, off + hp))
    in_specs = [zcol(0), zcol(nhp), zcol(2 * nhp),
                pl.BlockSpec((B, tb, 2 * LANES), lambda hp, t: (0, t, lora_blk)),
                vec(0), vec(nhp), vec(2 * nhp),
                pl.BlockSpec((1, 2 * LANES), lambda hp, t: (0, lora_blk)),
                vec(0), vec(0), vec(0), vec(0), vec(0), vec(0), vec(0),
                pl.BlockSpec((1, LANES, 2 * LANES), lambda hp, t: (hp, 0, 0)),
                pl.BlockSpec((R_GATE, LANES), lambda hp, t: (0, hp))]
    oa, s_pair = pl.pallas_call(
        _wkv_prompt_kernel,
        grid=(nhp, T // tb),
        in_specs=in_specs,
        out_specs=[pl.BlockSpec((B, tb, LANES), lambda hp, t: (0, t, hp)),
                   pl.BlockSpec((B, 1, LANES, LANES), lambda hp, t: (0, hp, 0, 0))],
        out_shape=[jax.ShapeDtypeStruct((B, T, DR), bf16),
                   jax.ShapeDtypeStruct((B, nhp, LANES, LANES), f32)],
        scratch_shapes=[pltpu.VMEM((B, LANES, LANES), f32),
                        pltpu.VMEM((B, 1, LANES), f32), pltpu.VMEM((B, 1, LANES), f32),
                        pltpu.VMEM((B, 1, LANES), f32), pltpu.VMEM((B, 1, 2 * LANES), f32),
                        pltpu.VMEM((B * tb, LANES), f32)],
        compiler_params=_cparams("parallel", "arbitrary"),
        name="wkv_prompt")(z3, z3, z3, z3, lp["mu"], lp["mu"], lp["mu"], lp["mu"],
                           lp["w_decay0"], lp["a0"], lp["k_k"], lp["k_a"], lp["r_k"],
                           lp["ln_x_w"], lp["ln_x_b"], lp["wda"], lp["wgu"])
    return oa.reshape(B * T, DR), s_pair


def _wkv_sample_prep_kernel(zr_ref, zk_ref, zv_ref, zl_ref, sr_ref, sk_ref, sv_ref, sl_ref,
                            mur_ref, muk_ref, muv_ref, mul_ref,
                            wd0_ref, a0_ref, kk_ref, ka_ref, rk_ref, wda_ref, wgu_ref,
                            r_out, w_out, kt_out, v_out, kap_out, b_out, g_out, bonus_out):
    def shifted(x_ref, p_ref, mu_ref):
        x = x_ref[...]
        return x + mu_ref[...] * (p_ref[...] - x)

    rs = shifted(zr_ref, sr_ref, mur_ref)
    ks = shifted(zk_ref, sk_ref, muk_ref)
    vs = shifted(zv_ref, sv_ref, muv_ref)
    ls = shifted(zl_ref, sl_ref, mul_ref)
    ones = _head_ones()
    lw, a, kap, kt, g, bonus = _wkv_prep(rs, ks, vs, ls, wd0_ref[...], a0_ref[...], kk_ref[...],
                                         ka_ref[...], rk_ref[...], wda_ref[0], wgu_ref[...], ones)
    r_out[...] = rs.T
    w_out[...] = jnp.exp(lw).T
    kt_out[...] = kt.T
    v_out[...] = vs.T
    kap_out[...] = kap.T
    b_out[...] = (kap * a).T
    g_out[...] = g
    bonus_out[...] = bonus


def _wkv_sample_prep(z, shift_prev, lp):
    M = z.shape[0]
    DR = lp["ln_x_w"].shape[1]
    nhp = DR // LANES
    lora_blk = 3 * DR // (2 * LANES)
    col = lambda off: pl.BlockSpec((M, LANES), lambda hp: (0, off + hp))
    vec = lambda off: pl.BlockSpec((1, LANES), lambda hp: (0, off + hp))
    wide = lambda rows: pl.BlockSpec((rows, 2 * LANES), lambda hp: (0, lora_blk))
    in_specs = [col(0), col(nhp), col(2 * nhp), wide(M),
                col(0), col(nhp), col(2 * nhp), wide(M),
                vec(0), vec(nhp), vec(2 * nhp), wide(1),
                vec(0), vec(0), vec(0), vec(0), vec(0),
                pl.BlockSpec((1, LANES, 2 * LANES), lambda hp: (hp, 0, 0)),
                pl.BlockSpec((R_GATE, LANES), lambda hp: (0, hp))]
    return pl.pallas_call(
        _wkv_sample_prep_kernel,
        grid=(nhp,),
        in_specs=in_specs,
        out_specs=[pl.BlockSpec((LANES, M), lambda hp: (hp, 0))] * 6 + [col(0)] * 2,
        out_shape=[jax.ShapeDtypeStruct((DR, M), f32)] * 6 + [jax.ShapeDtypeStruct((M, DR), f32)] * 2,
        compiler_params=_cparams("parallel"),
        name="wkv_sample_prep")(z, z, z, z, shift_prev, shift_prev, shift_prev, shift_prev,
                                lp["mu"], lp["mu"], lp["mu"], lp["mu"],
                                lp["w_decay0"], lp["a0"], lp["k_k"], lp["k_a"], lp["r_k"],
                                lp["wda"], lp["wgu"])


def _wkv_step_kernel(s_ref, w_ref, kap_ref, b_ref, kt_ref, r_ref, v_ref, so_ref, o_ref):
    for h in range(s_ref.shape[0]):
        s = s_ref[h]
        sa = jnp.sum(s * kap_ref[h][None], axis=1, keepdims=True)
        s_new = s * w_ref[h][None] - sa * b_ref[h][None] + v_ref[h] * kt_ref[h][None]
        so_ref[h] = s_new
        o_ref[h] = jnp.sum(s_new * r_ref[h][None], axis=1, keepdims=True)


def _wkv_step(s, w, kap, b, kt, r, v, li):
    _, H, _, _, nb = s.shape
    ht = _tile(H, 2, 1)
    mat = pl.BlockSpec((ht, HEAD_DIM, HEAD_DIM, nb), lambda i: (i, 0, 0, 0))
    vec = pl.BlockSpec((ht, HEAD_DIM, nb), lambda i: (i, 0, 0))
    col = pl.BlockSpec((ht, HEAD_DIM, 1, nb), lambda i: (i, 0, 0, 0))
    return pl.pallas_call(
        _wkv_step_kernel,
        grid=(H // ht,),
        in_specs=[pl.BlockSpec((None, ht, HEAD_DIM, HEAD_DIM, nb), lambda i: (li, i, 0, 0, 0))] + [vec] * 5 + [col],
        out_specs=[mat, col],
        out_shape=[jax.ShapeDtypeStruct(s.shape[1:], f32), jax.ShapeDtypeStruct((H, HEAD_DIM, 1, nb), f32)],
        compiler_params=_cparams("parallel"),
        name="wkv_step")(s, w, kap, b, kt, r, v)


def _wkv_sample_post_kernel(o_ref, bonus_ref, g_ref, lnw_ref, lnb_ref, oa_ref):
    oa_ref[...] = _wkv_post(o_ref[...].T, bonus_ref[...], g_ref[...], lnw_ref[...], lnb_ref[...], _head_ones())


def _wkv_sample_post(o, bonus, g, lp):
    DR, M = o.shape
    col = pl.BlockSpec((M, LANES), lambda hp: (0, hp))
    vec = pl.BlockSpec((1, LANES), lambda hp: (0, hp))
    return pl.pallas_call(
        _wkv_sample_post_kernel,
        grid=(DR // LANES,),
        in_specs=[pl.BlockSpec((LANES, M), lambda hp: (hp, 0)), col, col, vec, vec],
        out_specs=col,
        out_shape=jax.ShapeDtypeStruct((M, DR), bf16),
        compiler_params=_cparams("parallel"),
        name="wkv_sample_post")(o, bonus, g, lp["ln_x_w"], lp["ln_x_b"])


def _layer_params(i, DR, g_mix_pre, g_mix_post, g_ffn_pre, g_ffn_post, g_ple_post,
                  mu_shift, w_decay0, w_decay_up, a0, w_a_up, w_gate_up, k_k, k_a, r_k,
                  ln_x_w, ln_x_b, conv_w):
    nhp = DR // LANES
    row = lambda v: v[i].reshape(1, -1)
    wdu = w_decay_up[i].reshape(R_DECAY, nhp, LANES).transpose(1, 0, 2)
    wau = w_a_up[i].reshape(R_AAA, nhp, LANES).transpose(1, 0, 2)
    wda = jnp.concatenate([jnp.concatenate([wdu, jnp.zeros_like(wdu)], axis=2),
                           jnp.concatenate([jnp.zeros_like(wau), wau], axis=2)], axis=1).astype(bf16)
    return dict(
        g_mix_pre=row(g_mix_pre), g_mix_post=row(g_mix_post), g_ffn_pre=row(g_ffn_pre),
        g_ffn_post=row(g_ffn_post), g_ple_post=row(g_ple_post),
        mu=row(mu_shift), w_decay0=row(w_decay0), a0=row(a0), k_k=row(k_k), k_a=row(k_a),
        r_k=row(r_k), ln_x_w=row(ln_x_w), ln_x_b=row(ln_x_b), wda=wda, wgu=w_gate_up[i].astype(bf16),
        conv_w=conv_w[i])


def _mix_tail(x, p, zg, oa, ci, lp, wts, li):
    x = _mixout(oa, ci, wts["w_branch_a"], wts["w_branch_c"], zg, wts["w_out"], x, lp["g_mix_post"], li)
    x = _ffn(x, lp["g_ffn_pre"], wts["w_ffn_gate"], wts["w_ffn_up"], wts["w_ffn_down"], lp["g_ffn_post"], li)
    return _ple(x, p, wts["w_ple_in"], wts["w_ple_gate"], lp["g_ple_post"], li)


def kernel(x_prompt, x_sample, state_wkv, state_shift, state_conv, p_prompt, p_sample, g_mix_pre, g_mix_post, g_ffn_pre, g_ffn_post, g_ple_post, w_in, mu_shift, w_decay0, w_decay_up, a0, w_a_up, w_gate_up, k_k, k_a, r_k, ln_x_w, ln_x_b, conv_w, w_branch_a, w_branch_c, w_out, w_ffn_gate, w_ffn_up, w_ffn_down, w_ple_in, w_ple_gate):
    B, T, D = x_prompt.shape
    DB, DT, _ = x_sample.shape
    depth = w_in.shape[0]
    DR = w_decay0.shape[1]
    DC = conv_w.shape[2]
    H = DR // HEAD_DIM
    d_shift = mu_shift.shape[1]
    assert DT == 1 and T % CHUNK == 0 and T >= CONV_W - 1
    assert DR % (2 * LANES) == 0 and DC == DR and d_shift == 3 * DR + R_DECAY + R_AAA + R_GATE

    wts = dict(
        w_in=w_in.astype(bf16), w_branch_a=w_branch_a.astype(bf16), w_branch_c=w_branch_c.astype(bf16), w_out=w_out.astype(bf16),
        w_ffn_gate=w_ffn_gate.astype(bf16), w_ffn_up=w_ffn_up.astype(bf16),
        w_ffn_down=w_ffn_down.astype(bf16), w_ple_in=w_ple_in.astype(bf16),
        w_ple_gate=w_ple_gate.astype(bf16))
    pp = p_prompt.reshape(depth, B * T, -1)
    ps = p_sample.reshape(depth, DB, -1)
    s_t = jnp.transpose(state_wkv, (0, 2, 3, 4, 1))

    def inproj(x, lp, i):
        g, w = lp["g_mix_pre"], wts["w_in"]
        return (_inproj(x, g, w, i, 0, d_shift), _inproj(x, g, w, i, d_shift, 3 * DC),
                _inproj(x, g, w, i, d_shift + 3 * DC, 2 * D))

    xp = x_prompt.reshape(B * T, D)
    xs = x_sample.reshape(DB, D)
    outs = [[] for _ in range(6)]
    for i in range(depth):
        lp = _layer_params(i, DR, g_mix_pre, g_mix_post, g_ffn_pre, g_ffn_post, g_ple_post,
                           mu_shift, w_decay0, w_decay_up, a0, w_a_up, w_gate_up, k_k, k_a,
                           r_k.reshape(depth, DR), ln_x_w, ln_x_b, conv_w)

        zr, zc, zg = inproj(xp, lp, i)
        oa, s_pair = _wkv_prompt(zr, lp, B, T)
        ci, conv_p = _conv_prompt(zc, lp["conv_w"], B, T)
        xp = _mix_tail(xp, pp, zg, oa, ci, lp, wts, i)
        s_p = jnp.stack([s_pair[:, :, :HEAD_DIM, :HEAD_DIM], s_pair[:, :, HEAD_DIM:, HEAD_DIM:]], axis=2)
        outs[0].append(s_p.reshape(B, H, HEAD_DIM, HEAD_DIM))
        outs[1].append(zr.reshape(B, T, -1)[:, T - 1])
        outs[2].append(conv_p)

        zr, zc, zg = inproj(xs, lp, i)
        r, w, kt, v, kap, b, g, bonus = _wkv_sample_prep(zr, state_shift[i], lp)
        heads = lambda a_: a_.reshape(H, HEAD_DIM, DB)
        s_new, o = _wkv_step(s_t, heads(w), heads(kap), heads(b), heads(kt), heads(r),
                             v.reshape(H, HEAD_DIM, 1, DB), i)
        oa = _wkv_sample_post(o.reshape(DR, DB), bonus, g, lp)
        ci, u = _conv_sample(zc, lp["conv_w"], state_conv[i][:, 0], state_conv[i][:, 1])
        xs = _mix_tail(xs, ps, zg, oa, ci, lp, wts, i)
        outs[3].append(s_new)
        outs[4].append(zr)
        outs[5].append(jnp.stack([state_conv[i][:, 1], u], axis=1))

    outs = [jnp.stack(o) for o in outs]
    outs[3] = jnp.transpose(outs[3], (0, 4, 1, 2, 3))
    return (xp.reshape(B, T, D), xs.reshape(DB, 1, D)) + tuple(outs)
```

```python
import functools
import math

import jax
import jax.numpy as jnp
from jax import lax
from jax.experimental import pallas as pl
from jax.experimental.pallas import tpu as pltpu

HEAD_DIM = 64
LANES = 128
CHUNK = 64
R_DECAY = 64
R_AAA = 64
R_GATE = 128
CONV_W = 3
RMS_EPS = 1e-6
GN_EPS = 64e-5
VMEM_LIMIT_MB = 56

f32 = jnp.float32
bf16 = jnp.bfloat16


def _cparams(*sem):
    return pltpu.CompilerParams(dimension_semantics=sem, vmem_limit_bytes=VMEM_LIMIT_MB * 2**20)


def _tile(n, pref, mult=8):
    if n <= pref:
        return n
    for t in range(pref - pref % mult, 0, -mult):
        if n % t == 0:
            return t
    return n


def _dot(a, b):
    return jnp.dot(a, b, preferred_element_type=f32)


def _dot_nt(a, b):
    return lax.dot_general(a, b, (((1,), (1,)), ((), ())), preferred_element_type=f32)


def _dot_tn(a, b):
    return lax.dot_general(a, b, (((0,), (0,)), ((), ())), preferred_element_type=f32)


def _rms_unit(x):
    return x * lax.rsqrt(jnp.mean(x * x, axis=-1, keepdims=True) + RMS_EPS)


def _inproj_kernel(x_ref, g_ref, w_ref, o_ref, h_ref):
    @pl.when(pl.program_id(1) == 0)
    def _():
        h_ref[...] = (_rms_unit(x_ref[...]) * g_ref[...]).astype(bf16)
    o_ref[...] = _dot(h_ref[...], w_ref[0])


def _inproj(x, g, w, li, segs):
    M, D = x.shape
    widths = [b - a for a, b in segs]
    NZ = sum(widths)
    tm, tn = _tile(M, 1024), _tile(math.gcd(*widths), 1024, LANES)
    assert all(a % LANES == 0 for a, _ in segs)

    def src_col(j):
        col, out_start, shift = j * tn, 0, 0
        for (a, _), wd in zip(segs, widths):
            col = col + jnp.where(j * tn >= out_start, (a - out_start) - shift, 0)
            out_start, shift = out_start + wd, a - out_start
        return pl.multiple_of(col, LANES)

    return pl.pallas_call(
        _inproj_kernel,
        grid=(M // tm, NZ // tn),
        in_specs=[pl.BlockSpec((tm, D), lambda i, j: (i, 0)),
                  pl.BlockSpec((1, D), lambda i, j: (0, 0)),
                  pl.BlockSpec((pl.Element(1), pl.Element(D), pl.Element(tn)),
                               lambda i, j: (li, 0, src_col(j)))],
        out_specs=pl.BlockSpec((tm, tn), lambda i, j: (i, j)),
        out_shape=jax.ShapeDtypeStruct((M, NZ), f32),
        scratch_shapes=[pltpu.VMEM((tm, D), bf16)],
        compiler_params=_cparams("parallel", "arbitrary"),
        name="inproj")(x, g, w)


def _mixout_kernel(oa_ref, ci_ref, wa_ref, wc_ref, ga_ref, gc_ref, wo_ref, x_ref, g_ref, gn_ref, o_ref, h_ref):
    ya = _dot(oa_ref[...], wa_ref[...])
    yc = _dot(ci_ref[...], wc_ref[...])
    merged = jax.nn.sigmoid(ga_ref[...]) * ya + jax.nn.sigmoid(gc_ref[...]) * yc
    y = _dot(merged.astype(bf16), wo_ref[...])
    x = x_ref[...] + _rms_unit(y) * g_ref[...]
    o_ref[...] = x
    h_ref[...] = (_rms_unit(x) * gn_ref[...]).astype(bf16)


def _mixout(oa, ci, wa, wc, z, gate_off, wo, x, g, g_next, li):
    M, D = x.shape
    DC = oa.shape[1]
    tm = _tile(M, 256)
    g0 = gate_off // D
    resident = lambda shape: pl.BlockSpec((None,) + shape, lambda i: (li, 0, 0), pipeline_mode=pl.Buffered(1))
    return pl.pallas_call(
        _mixout_kernel,
        grid=(M // tm,),
        in_specs=[pl.BlockSpec((tm, DC), lambda i: (i, 0)),
                  pl.BlockSpec((tm, DC), lambda i: (i, 0)),
                  resident((DC, D)), resident((DC, D)),
                  pl.BlockSpec((tm, D), lambda i: (i, g0)),
                  pl.BlockSpec((tm, D), lambda i: (i, g0 + 1)),
                  resident((D, D)),
                  pl.BlockSpec((tm, D), lambda i: (i, 0)),
                  pl.BlockSpec((1, D), lambda i: (0, 0)),
                  pl.BlockSpec((1, D), lambda i: (0, 0))],
        out_specs=[pl.BlockSpec((tm, D), lambda i: (i, 0)), pl.BlockSpec((tm, D), lambda i: (i, 0))],
        out_shape=[jax.ShapeDtypeStruct((M, D), f32), jax.ShapeDtypeStruct((M, D), bf16)],
        compiler_params=_cparams("parallel"),
        name="mixout")(oa, ci, wa, wc, z, z, wo, x, g, g_next)


def _ffn_kernel(h_ref, wg_ref, wu_ref, wd_ref, g2_ref, o_ref):
    j = pl.program_id(1)

    @pl.when(j == 0)
    def _():
        o_ref[...] = jnp.zeros_like(o_ref)

    h = h_ref[...]
    a = _dot(h, wg_ref[...])
    u = _dot(h, wu_ref[...])
    act = (a * jax.nn.sigmoid(a) * u).astype(bf16)
    o_ref[...] += _dot(act, wd_ref[...])

    @pl.when(j == pl.num_programs(1) - 1)
    def _():
        o_ref[...] = _rms_unit(o_ref[...]) * g2_ref[...]


def _ffn(h, wg, wu, wd, g2, li):
    M, D = h.shape
    F = wg.shape[2]
    tm, tf = _tile(M, 1024), _tile(F, 512, LANES)
    return pl.pallas_call(
        _ffn_kernel,
        grid=(M // tm, F // tf),
        in_specs=[pl.BlockSpec((tm, D), lambda i, j: (i, 0)),
                  pl.BlockSpec((None, D, tf), lambda i, j: (li, 0, j)),
                  pl.BlockSpec((None, D, tf), lambda i, j: (li, 0, j)),
                  pl.BlockSpec((None, tf, D), lambda i, j: (li, j, 0)),
                  pl.BlockSpec((1, D), lambda i, j: (0, 0))],
        out_specs=pl.BlockSpec((tm, D), lambda i, j: (i, 0)),
        out_shape=jax.ShapeDtypeStruct((M, D), f32),
        compiler_params=_cparams("parallel", "arbitrary"),
        name="ffn")(h, wg, wu, wd, g2)


def _ple_kernel(x_ref, f_ref, p_ref, wi_ref, wg_ref, g_ref, o_ref):
    x = x_ref[...] + f_ref[...]
    e = _dot(p_ref[...].astype(bf16), wi_ref[...]) * jax.nn.sigmoid(
        _dot(_rms_unit(x).astype(bf16), wg_ref[...]))
    o_ref[...] = x + _rms_unit(e) * g_ref[...]


def _ple(x, f, p, wi, wg, g, li):
    M, D = x.shape
    P = p.shape[2]
    tm = _tile(M, 512)
    return pl.pallas_call(
        _ple_kernel,
        grid=(M // tm,),
        in_specs=[pl.BlockSpec((tm, D), lambda i: (i, 0)),
                  pl.BlockSpec((tm, D), lambda i: (i, 0)),
                  pl.BlockSpec((None, tm, P), lambda i: (li, i, 0)),
                  pl.BlockSpec((None, P, D), lambda i: (li, 0, 0)),
                  pl.BlockSpec((None, D, D), lambda i: (li, 0, 0)),
                  pl.BlockSpec((1, D), lambda i: (0, 0))],
        out_specs=pl.BlockSpec((tm, D), lambda i: (i, 0)),
        out_shape=jax.ShapeDtypeStruct((M, D), f32),
        compiler_params=_cparams("parallel"),
        name="ple")(x, f, p, wi, wg, g)


def _conv_prompt_kernel(cb_ref, cc_ref, ch_ref, ccp_ref, chp_ref, w_ref, o_ref, st_ref, *, tiles_per_seq):
    tm = cb_ref.shape[0]
    first = pl.program_id(0) % tiles_per_seq == 0
    u = cc_ref[...] * ch_ref[...]
    up = ccp_ref[...] * chp_ref[...]
    p0 = jnp.where(first, 0.0, up[6:7])
    p1 = jnp.where(first, 0.0, up[7:8])
    row = lax.broadcasted_iota(jnp.int32, u.shape, 0)
    u1 = jnp.where(row == 0, p1, pltpu.roll(u, 1, 0))
    u2 = jnp.where(row == 0, p0, jnp.where(row == 1, p1, pltpu.roll(u, 2, 0)))
    conv = w_ref[0:1] * u2 + w_ref[1:2] * u1 + w_ref[2:3] * u
    o_ref[...] = (cb_ref[...] * conv).astype(bf16)
    st_ref[0] = u[tm - (CONV_W - 1):tm]


def _conv_prompt(z, conv_off, conv_w, B, T):
    M = z.shape[0]
    DC = conv_w.shape[1]
    tm = _tile(T, 512)
    c0 = conv_off // DC
    prev = lambda i: (jnp.maximum(i * (tm // 8) - 1, 0))
    return pl.pallas_call(
        functools.partial(_conv_prompt_kernel, tiles_per_seq=T // tm),
        grid=(M // tm,),
        in_specs=[pl.BlockSpec((tm, DC), lambda i: (i, c0)),
                  pl.BlockSpec((tm, DC), lambda i: (i, c0 + 1)),
                  pl.BlockSpec((tm, DC), lambda i: (i, c0 + 2)),
                  pl.BlockSpec((8, DC), lambda i: (prev(i), c0 + 1)),
                  pl.BlockSpec((8, DC), lambda i: (prev(i), c0 + 2)),
                  pl.BlockSpec((CONV_W, DC), lambda i: (0, 0))],
        out_specs=[pl.BlockSpec((tm, DC), lambda i: (i, 0)),
                   pl.BlockSpec((1, CONV_W - 1, DC), lambda i: (i // (T // tm), 0, 0))],
        out_shape=[jax.ShapeDtypeStruct((M, DC), bf16),
                   jax.ShapeDtypeStruct((B, CONV_W - 1, DC), f32)],
        compiler_params=_cparams("arbitrary"),
        name="conv_prompt")(z, z, z, z, z, conv_w)


def _conv_sample_kernel(cb_ref, cc_ref, ch_ref, b0_ref, b1_ref, w_ref, o_ref, u_ref):
    u = cc_ref[...] * ch_ref[...]
    conv = w_ref[0:1] * b0_ref[...] + w_ref[1:2] * b1_ref[...] + w_ref[2:3] * u
    o_ref[...] = (cb_ref[...] * conv).astype(bf16)
    u_ref[...] = u


def _conv_sample(z, conv_off, conv_w, buf0, buf1):
    M = z.shape[0]
    DC = conv_w.shape[1]
    c0 = conv_off // DC
    col = lambda c: pl.BlockSpec((M, DC), lambda i: (0, c))
    return pl.pallas_call(
        _conv_sample_kernel,
        grid=(1,),
        in_specs=[col(c0), col(c0 + 1), col(c0 + 2), col(0), col(0),
                  pl.BlockSpec((CONV_W, DC), lambda i: (0, 0))],
        out_specs=[col(0), col(0)],
        out_shape=[jax.ShapeDtypeStruct((M, DC), bf16), jax.ShapeDtypeStruct((M, DC), f32)],
        compiler_params=_cparams("arbitrary"),
        name="conv_sample")(z, z, z, buf0, buf1, conv_w)


def _head_ones():
    r = lax.broadcasted_iota(jnp.int32, (2 * LANES, 2 * LANES), 0) >> 6
    c = lax.broadcasted_iota(jnp.int32, (2 * LANES, 2 * LANES), 1) >> 6
    return (r == c).astype(bf16)


def _head_sum(x, ones):
    hi = x.astype(bf16)
    lo = (x - hi.astype(f32)).astype(bf16)
    s = _dot(jnp.concatenate([hi, lo], axis=1), ones)
    return s[:, :LANES] + s[:, LANES:]


def _wkv_prep(rs, ks, vs, ls, wd0, a0, k_k, k_a, r_k, wda, wgu, ones):
    lane = lax.broadcasted_iota(jnp.int32, rs.shape, 1)
    l0 = ls[:, :LANES]
    lin = jnp.where(lane < R_DECAY, jnp.tanh(l0), l0).astype(bf16)
    da = _dot(lin, wda)
    lw = -math.exp(-0.5) * jax.nn.sigmoid(wd0 + da[:, :LANES])
    a = jax.nn.sigmoid(a0 + da[:, LANES:])
    g = _dot(jax.nn.sigmoid(ls[:, LANES:]).astype(bf16), wgu)
    kkr = ks * k_k
    kap = kkr / jnp.maximum(jnp.sqrt(_head_sum(kkr * kkr, ones)), 1e-12)
    kt = ks * (1.0 + (a - 1.0) * k_a)
    bonus = _head_sum(rs * kt * r_k, ones) * vs
    return lw, a, kap, kt, g, bonus


def _wkv_post(o, bonus, g, ln_w, ln_b, ones):
    inv = 1.0 / HEAD_DIM
    d = o - _head_sum(o, ones) * inv
    var = _head_sum(d * d, ones) * inv
    on = d * lax.rsqrt(var + GN_EPS) * ln_w + ln_b
    return ((on + bonus) * g).astype(bf16)


def _wkv_prompt_kernel(zr_ref, zk_ref, zv_ref, zl_ref, mur_ref, muk_ref, muv_ref, mul_ref,
                       wd0_ref, a0_ref, kk_ref, ka_ref, rk_ref, lnw_ref, lnb_ref, wda_ref, wgu_ref,
                       o_ref, s_ref, st_ref, pr_ref, pk_ref, pv_ref, pl_ref):
    t = pl.program_id(1)
    nb, tb, _ = zr_ref.shape
    rows = nb * tb

    @pl.when(t == 0)
    def _():
        st_ref[...] = jnp.zeros_like(st_ref)
        pr_ref[...] = jnp.zeros_like(pr_ref)
        pk_ref[...] = jnp.zeros_like(pk_ref)
        pv_ref[...] = jnp.zeros_like(pv_ref)
        pl_ref[...] = jnp.zeros_like(pl_ref)

    def shifted(x_ref, p_ref, mu_ref):
        x = x_ref[...]
        w = x.shape[2]
        rolled = pltpu.roll(x.reshape(rows, w), 1, 0).reshape(nb, tb, w)
        row = lax.broadcasted_iota(jnp.int32, x.shape, 1)
        xp = jnp.where(row == 0, p_ref[...], rolled)
        p_ref[...] = x[:, tb - 1:tb, :]
        return (x + mu_ref[...] * (xp - x)).reshape(rows, w)

    rs = shifted(zr_ref, pr_ref, mur_ref)
    ks = shifted(zk_ref, pk_ref, muk_ref)
    vs = shifted(zv_ref, pv_ref, muv_ref)
    ls = shifted(zl_ref, pl_ref, mul_ref)
    ones = _head_ones()
    lw, a, kap, kt, g, bonus = _wkv_prep(rs, ks, vs, ls, wd0_ref[...], a0_ref[...], kk_ref[...],
                                         ka_ref[...], rk_ref[...], wda_ref[0], wgu_ref[...], ones)
    b = kap * a

    ri = lax.broadcasted_iota(jnp.int32, (tb, tb), 0)
    ci = lax.broadcasted_iota(jnp.int32, (tb, tb), 1)
    tri = (((ri >> 6) == (ci >> 6)) & (ci <= ri)).astype(bf16)
    h1 = lw.astype(bf16)
    h2 = (lw - h1.astype(f32)).astype(bf16)
    hl = jnp.concatenate([h1, h2], axis=1)
    cums = []
    for i in range(nb):
        cs = _dot(tri, hl[i * tb:(i + 1) * tb])
        cums.append(cs[:, :LANES] + cs[:, LANES:])

    rr = lax.broadcasted_iota(jnp.int32, (LANES, LANES), 0)
    cc = lax.broadcasted_iota(jnp.int32, (LANES, LANES), 1)
    same_head = (rr >> 6) == (cc >> 6)
    tt, ss = rr & (CHUNK - 1), cc & (CHUNK - 1)
    strict = same_head & (ss < tt)
    incl = same_head & (ss <= tt)
    eye = rr == cc

    def level_mask(k):
        rb, cb = tt >> k, ss >> k
        return same_head & ((rb >> 1) == (cb >> 1)) & ((rb & 1) == 1) & ((cb & 1) == 0)

    def pair(x):
        return jnp.where(same_head, jnp.concatenate([x, x], axis=0), 0.0)

    ncs = tb // CHUNK
    units = [(i, c) for i in range(nb) for c in range(ncs)]
    n = len(units)
    kh, rh, vb, bd, kd, x, y, e_last = [], [], [], [], [], [], [], []
    for i, c in units:
        sl = slice(i * tb + c * CHUNK, i * tb + (c + 1) * CHUNK)
        cm = cums[i][c * CHUNK:(c + 1) * CHUNK]
        c_last = cums[i][(c + 1) * CHUNK - 1:(c + 1) * CHUNK]
        e_p, e_m = jnp.exp(cm), jnp.exp(-cm)
        e_x, e_d = jnp.exp(cm - lw[sl]), jnp.exp(c_last - cm)
        kh.append(pair(kap[sl] * e_x).astype(bf16))
        rh.append(pair(rs[sl] * e_p))
        vb.append(pair(vs[sl]).astype(bf16))
        bd.append(pair(b[sl] * e_d).astype(bf16))
        kd.append(pair(kt[sl] * e_d).astype(bf16))
        x.append(jnp.concatenate([kh[-1], rh[-1].astype(bf16)], axis=0))
        ybc, ykc = (b[sl] * e_m).astype(bf16), (kt[sl] * e_m).astype(bf16)
        y.append(jnp.concatenate([ybc, ybc, ykc, ykc], axis=0))
        e_last.append(jnp.exp(c_last))

    each = lambda f: [f(u) for u in range(n)]
    att = each(lambda u: _dot_nt(x[u], y[u]))
    low = each(lambda u: jnp.where(strict, att[u][:LANES, :LANES], 0.0))
    lowb = each(lambda u: low[u].astype(bf16))
    akk = each(lambda u: jnp.where(strict, att[u][:LANES, LANES:], 0.0).astype(bf16))
    arb_neg = each(lambda u: jnp.where(incl, -att[u][LANES:, :LANES], 0.0).astype(bf16))
    ark = each(lambda u: jnp.where(incl, att[u][LANES:, LANES:], 0.0).astype(bf16))

    m0 = level_mask(0)
    inv = each(lambda u: jnp.where(eye, 1.0, 0.0) - jnp.where(m0, low[u], 0.0))
    for k in range(1, 6):
        mk = level_mask(k)
        ib = each(lambda u: inv[u].astype(bf16))
        t1 = each(lambda u: _dot(ib[u], jnp.where(mk, lowb[u], 0.0)).astype(bf16))
        inv = each(lambda u: inv[u] - _dot(t1[u], ib[u]))
    w = each(lambda u: inv[u].astype(bf16))

    p = each(lambda u: _dot(akk[u], vb[u]).astype(bf16))
    uk = each(lambda u: _dot(w[u], jnp.concatenate([p[u], kh[u]], axis=1)).astype(bf16))
    zero = jnp.zeros((LANES, LANES), bf16)
    q = each(lambda u: _dot(jnp.concatenate([ark[u], arb_neg[u]], axis=1),
                            jnp.concatenate([jnp.concatenate([vb[u], zero], axis=1), uk[u]], axis=0)))
    o0 = each(lambda u: q[u][:, :LANES])
    rbar = each(lambda u: (rh[u] + q[u][:, LANES:]).astype(bf16))
    g0t = each(lambda u: _dot_tn(jnp.concatenate([vb[u], -uk[u][:, :LANES]], axis=0),
                                 jnp.concatenate([kd[u], bd[u]], axis=0)))
    phit = each(lambda u: (jnp.where(eye, e_last[u], 0.0) - _dot_tn(uk[u][:, LANES:], bd[u])).astype(bf16))

    s = [st_ref[i] for i in range(nb)]
    o = [None] * n
    for c in range(ncs):
        for i in range(nb):
            u = i * ncs + c
            sb = s[i].astype(bf16)
            ou = _dot_nt(rbar[u], sb) + o0[u]
            s[i] = _dot(sb, phit[u]) + g0t[u]
            o[u] = ou[:CHUNK] + ou[CHUNK:]
    for i in range(nb):
        st_ref[i] = s[i]

    oa = _wkv_post(jnp.concatenate(o, axis=0), bonus, g, lnw_ref[...], lnb_ref[...], ones)
    o_ref[...] = oa.reshape(nb, tb, LANES)

    @pl.when(t == pl.num_programs(1) - 1)
    def _():
        s_ref[:, 0] = st_ref[...]


def _wkv_prompt(z, zl, lp, B, T):
    DR = lp["ln_x_w"].shape[1]
    nhp = DR // LANES
    tb = _tile(T, 4 * CHUNK, CHUNK)
    lora_blk = 3 * DR // (2 * LANES)
    z3 = z.reshape(B, T, -1)
    zcol = lambda off: pl.BlockSpec((B, tb, LANES), lambda hp, t: (0, t, off + hp))
    vec = lambda off: pl.BlockSpec((1, LANES), lambda hp, t: (0, off + hp))
    in_specs = [zcol(0), zcol(nhp), zcol(2 * nhp),
                pl.BlockSpec((B, tb, 2 * LANES), lambda hp, t: (0, t, 0)),
                vec(0), vec(nhp), vec(2 * nhp),
                pl.BlockSpec((1, 2 * LANES), lambda hp, t: (0, lora_blk)),
                vec(0), vec(0), vec(0), vec(0), vec(0), vec(0), vec(0),
                pl.BlockSpec((1, LANES, 2 * LANES), lambda hp, t: (hp, 0, 0)),
                pl.BlockSpec((R_GATE, LANES), lambda hp, t: (0, hp))]
    oa, s_pair = pl.pallas_call(
        _wkv_prompt_kernel,
        grid=(nhp, T // tb),
        in_specs=in_specs,
        out_specs=[pl.BlockSpec((B, tb, LANES), lambda hp, t: (0, t, hp)),
                   pl.BlockSpec((B, 1, LANES, LANES), lambda hp, t: (0, hp, 0, 0))],
        out_shape=[jax.ShapeDtypeStruct((B, T, DR), bf16),
                   jax.ShapeDtypeStruct((B, nhp, LANES, LANES), f32)],
        scratch_shapes=[pltpu.VMEM((B, LANES, LANES), f32),
                        pltpu.VMEM((B, 1, LANES), f32), pltpu.VMEM((B, 1, LANES), f32),
                        pltpu.VMEM((B, 1, LANES), f32), pltpu.VMEM((B, 1, 2 * LANES), f32)],
        compiler_params=_cparams("parallel", "arbitrary"),
        name="wkv_prompt")(z3, z3, z3, zl.reshape(B, T, -1), lp["mu"], lp["mu"], lp["mu"], lp["mu"],
                           lp["w_decay0"], lp["a0"], lp["k_k"], lp["k_a"], lp["r_k"],
                           lp["ln_x_w"], lp["ln_x_b"], lp["wda"], lp["wgu"])
    return oa.reshape(B * T, DR), s_pair


def _wkv_sample_prep_kernel(zr_ref, zk_ref, zv_ref, zl_ref, sr_ref, sk_ref, sv_ref, sl_ref,
                            mur_ref, muk_ref, muv_ref, mul_ref,
                            wd0_ref, a0_ref, kk_ref, ka_ref, rk_ref, wda_ref, wgu_ref,
                            r_out, w_out, kt_out, v_out, kap_out, b_out, g_out, bonus_out):
    def shifted(x_ref, p_ref, mu_ref):
        x = x_ref[...]
        return x + mu_ref[...] * (p_ref[...] - x)

    rs = shifted(zr_ref, sr_ref, mur_ref)
    ks = shifted(zk_ref, sk_ref, muk_ref)
    vs = shifted(zv_ref, sv_ref, muv_ref)
    ls = shifted(zl_ref, sl_ref, mul_ref)
    ones = _head_ones()
    lw, a, kap, kt, g, bonus = _wkv_prep(rs, ks, vs, ls, wd0_ref[...], a0_ref[...], kk_ref[...],
                                         ka_ref[...], rk_ref[...], wda_ref[0], wgu_ref[...], ones)
    r_out[...] = rs.T
    w_out[...] = jnp.exp(lw).T
    kt_out[...] = kt.T
    v_out[...] = vs.T
    kap_out[...] = kap.T
    b_out[...] = (kap * a).T
    g_out[...] = g
    bonus_out[...] = bonus


def _wkv_sample_prep(z, zl, shift_prev, lp):
    M = z.shape[0]
    DR = lp["ln_x_w"].shape[1]
    nhp = DR // LANES
    lora_blk = 3 * DR // (2 * LANES)
    col = lambda off: pl.BlockSpec((M, LANES), lambda hp: (0, off + hp))
    vec = lambda off: pl.BlockSpec((1, LANES), lambda hp: (0, off + hp))
    wide = lambda rows: pl.BlockSpec((rows, 2 * LANES), lambda hp: (0, lora_blk))
    in_specs = [col(0), col(nhp), col(2 * nhp), pl.BlockSpec((M, 2 * LANES), lambda hp: (0, 0)),
                col(0), col(nhp), col(2 * nhp), wide(M),
                vec(0), vec(nhp), vec(2 * nhp), wide(1),
                vec(0), vec(0), vec(0), vec(0), vec(0),
                pl.BlockSpec((1, LANES, 2 * LANES), lambda hp: (hp, 0, 0)),
                pl.BlockSpec((R_GATE, LANES), lambda hp: (0, hp))]
    return pl.pallas_call(
        _wkv_sample_prep_kernel,
        grid=(nhp,),
        in_specs=in_specs,
        out_specs=[pl.BlockSpec((LANES, M), lambda hp: (hp, 0))] * 6 + [col(0)] * 2,
        out_shape=[jax.ShapeDtypeStruct((DR, M), f32)] * 6 + [jax.ShapeDtypeStruct((M, DR), f32)] * 2,
        compiler_params=_cparams("parallel"),
        name="wkv_sample_prep")(z, z, z, zl, shift_prev, shift_prev, shift_prev, shift_prev,
                                lp["mu"], lp["mu"], lp["mu"], lp["mu"],
                                lp["w_decay0"], lp["a0"], lp["k_k"], lp["k_a"], lp["r_k"],
                                lp["wda"], lp["wgu"])


def _wkv_step_kernel(s_ref, w_ref, kap_ref, b_ref, kt_ref, r_ref, v_ref, so_ref, o_ref):
    for h in range(s_ref.shape[0]):
        s = s_ref[h]
        sa = jnp.sum(s * kap_ref[h][None], axis=1, keepdims=True)
        s_new = s * w_ref[h][None] - sa * b_ref[h][None] + v_ref[h] * kt_ref[h][None]
        so_ref[h] = s_new
        o_ref[h] = jnp.sum(s_new * r_ref[h][None], axis=1, keepdims=True)


def _wkv_step(s, w, kap, b, kt, r, v, li):
    _, H, _, _, nb = s.shape
    ht = _tile(H, 2, 1)
    mat = pl.BlockSpec((ht, HEAD_DIM, HEAD_DIM, nb), lambda i: (i, 0, 0, 0))
    vec = pl.BlockSpec((ht, HEAD_DIM, nb), lambda i: (i, 0, 0))
    col = pl.BlockSpec((ht, HEAD_DIM, 1, nb), lambda i: (i, 0, 0, 0))
    return pl.pallas_call(
        _wkv_step_kernel,
        grid=(H // ht,),
        in_specs=[pl.BlockSpec((None, ht, HEAD_DIM, HEAD_DIM, nb), lambda i: (li, i, 0, 0, 0))] + [vec] * 5 + [col],
        out_specs=[mat, col],
        out_shape=[jax.ShapeDtypeStruct(s.shape[1:], f32), jax.ShapeDtypeStruct((H, HEAD_DIM, 1, nb), f32)],
        compiler_params=_cparams("parallel"),
        name="wkv_step")(s, w, kap, b, kt, r, v)


def _wkv_sample_post_kernel(o_ref, bonus_ref, g_ref, lnw_ref, lnb_ref, oa_ref):
    oa_ref[...] = _wkv_post(o_ref[...].T, bonus_ref[...], g_ref[...], lnw_ref[...], lnb_ref[...], _head_ones())


def _wkv_sample_post(o, bonus, g, lp):
    DR, M = o.shape
    col = pl.BlockSpec((M, LANES), lambda hp: (0, hp))
    vec = pl.BlockSpec((1, LANES), lambda hp: (0, hp))
    return pl.pallas_call(
        _wkv_sample_post_kernel,
        grid=(DR // LANES,),
        in_specs=[pl.BlockSpec((LANES, M), lambda hp: (hp, 0)), col, col, vec, vec],
        out_specs=col,
        out_shape=jax.ShapeDtypeStruct((M, DR), bf16),
        compiler_params=_cparams("parallel"),
        name="wkv_sample_post")(o, bonus, g, lp["ln_x_w"], lp["ln_x_b"])


def _layer_params(i, DR, g_mix_pre, g_mix_post, g_ffn_pre, g_ffn_post, g_ple_post,
                  mu_shift, w_decay0, w_decay_up, a0, w_a_up, w_gate_up, k_k, k_a, r_k,
                  ln_x_w, ln_x_b, conv_w):
    nhp = DR // LANES
    row = lambda v: v[i].reshape(1, -1)
    wdu = w_decay_up[i].reshape(R_DECAY, nhp, LANES).transpose(1, 0, 2)
    wau = w_a_up[i].reshape(R_AAA, nhp, LANES).transpose(1, 0, 2)
    wda = jnp.concatenate([jnp.concatenate([wdu, jnp.zeros_like(wdu)], axis=2),
                           jnp.concatenate([jnp.zeros_like(wau), wau], axis=2)], axis=1).astype(bf16)
    return dict(
        g_mix_pre=row(g_mix_pre), g_mix_post=row(g_mix_post), g_ffn_pre=row(g_ffn_pre),
        g_ffn_post=row(g_ffn_post), g_ple_post=row(g_ple_post),
        mu=row(mu_shift), w_decay0=row(w_decay0), a0=row(a0), k_k=row(k_k), k_a=row(k_a),
        r_k=row(r_k), ln_x_w=row(ln_x_w), ln_x_b=row(ln_x_b), wda=wda, wgu=w_gate_up[i].astype(bf16),
        conv_w=conv_w[i])


def _mix_tail(x, p, z, gate_off, oa, ci, lp, wts, li):
    x, h = _mixout(oa, ci, wts["w_branch_a"], wts["w_branch_c"], z, gate_off, wts["w_out"], x,
                   lp["g_mix_post"], lp["g_ffn_pre"], li)
    f = _ffn(h, wts["w_ffn_gate"], wts["w_ffn_up"], wts["w_ffn_down"], lp["g_ffn_post"], li)
    return _ple(x, f, p, wts["w_ple_in"], wts["w_ple_gate"], lp["g_ple_post"], li)


def kernel(x_prompt, x_sample, state_wkv, state_shift, state_conv, p_prompt, p_sample, g_mix_pre, g_mix_post, g_ffn_pre, g_ffn_post, g_ple_post, w_in, mu_shift, w_decay0, w_decay_up, a0, w_a_up, w_gate_up, k_k, k_a, r_k, ln_x_w, ln_x_b, conv_w, w_branch_a, w_branch_c, w_out, w_ffn_gate, w_ffn_up, w_ffn_down, w_ple_in, w_ple_gate):
    B, T, D = x_prompt.shape
    DB, DT, _ = x_sample.shape
    depth = w_in.shape[0]
    DR = w_decay0.shape[1]
    DC = conv_w.shape[2]
    H = DR // HEAD_DIM
    d_shift = mu_shift.shape[1]
    assert DT == 1 and T % CHUNK == 0 and T >= CONV_W - 1
    assert DR % (2 * LANES) == 0 and DC == DR and d_shift == 3 * DR + R_DECAY + R_AAA + R_GATE

    wts = dict(
        w_in=w_in.astype(bf16), w_branch_a=w_branch_a.astype(bf16), w_branch_c=w_branch_c.astype(bf16), w_out=w_out.astype(bf16),
        w_ffn_gate=w_ffn_gate.astype(bf16), w_ffn_up=w_ffn_up.astype(bf16),
        w_ffn_down=w_ffn_down.astype(bf16), w_ple_in=w_ple_in.astype(bf16),
        w_ple_gate=w_ple_gate.astype(bf16))
    pp = p_prompt.reshape(depth, B * T, -1)
    ps = p_sample.reshape(depth, DB, -1)
    s_t = jnp.transpose(state_wkv, (0, 2, 3, 4, 1))

    conv_off, gate_off = 3 * DR, 3 * DR + 3 * DC

    def inproj(x, lp, i):
        g, w = lp["g_mix_pre"], wts["w_in"]
        return (_inproj(x, g, w, i, [(0, 3 * DR), (d_shift, w_in.shape[2])]),
                _inproj(x, g, w, i, [(3 * DR, d_shift)]))

    xp = x_prompt.reshape(B * T, D)
    xs = x_sample.reshape(DB, D)
    outs = [[] for _ in range(6)]
    for i in range(depth):
        lp = _layer_params(i, DR, g_mix_pre, g_mix_post, g_ffn_pre, g_ffn_post, g_ple_post,
                           mu_shift, w_decay0, w_decay_up, a0, w_a_up, w_gate_up, k_k, k_a,
                           r_k.reshape(depth, DR), ln_x_w, ln_x_b, conv_w)

        z, zl = inproj(xp, lp, i)
        oa, s_pair = _wkv_prompt(z, zl, lp, B, T)
        ci, conv_p = _conv_prompt(z, conv_off, lp["conv_w"], B, T)
        xp = _mix_tail(xp, pp, z, gate_off, oa, ci, lp, wts, i)
        s_p = jnp.stack([s_pair[:, :, :HEAD_DIM, :HEAD_DIM], s_pair[:, :, HEAD_DIM:, HEAD_DIM:]], axis=2)
        outs[0].append(s_p.reshape(B, H, HEAD_DIM, HEAD_DIM))
        outs[1].append(jnp.concatenate([z.reshape(B, T, -1)[:, T - 1, :3 * DR], zl.reshape(B, T, -1)[:, T - 1]], axis=1))
        outs[2].append(conv_p)

        z, zl = inproj(xs, lp, i)
        r, w, kt, v, kap, b, g, bonus = _wkv_sample_prep(z, zl, state_shift[i], lp)
        heads = lambda a_: a_.reshape(H, HEAD_DIM, DB)
        s_new, o = _wkv_step(s_t, heads(w), heads(kap), heads(b), heads(kt), heads(r),
                             v.reshape(H, HEAD_DIM, 1, DB), i)
        oa = _wkv_sample_post(o.reshape(DR, DB), bonus, g, lp)
        ci, u = _conv_sample(z, conv_off, lp["conv_w"], state_conv[i][:, 0], state_conv[i][:, 1])
        xs = _mix_tail(xs, ps, z, gate_off, oa, ci, lp, wts, i)
        outs[3].append(s_new)
        outs[4].append(jnp.concatenate([z[:, :3 * DR], zl], axis=1))
        outs[5].append(jnp.stack([state_conv[i][:, 1], u], axis=1))

    outs = [jnp.stack(o) for o in outs]
    outs[3] = jnp.transpose(outs[3], (0, 4, 1, 2, 3))
    return (xp.reshape(B, T, D), xs.reshape(DB, 1, D)) + tuple(outs)
```

```python
import functools
import math

import jax
import jax.numpy as jnp
from jax import lax
from jax.experimental import pallas as pl
from jax.experimental.pallas import tpu as pltpu

HEAD_DIM = 64
LANES = 128
CHUNK = 64
R_DECAY = 64
R_AAA = 64
R_GATE = 128
CONV_W = 3
RMS_EPS = 1e-6
GN_EPS = 64e-5
VMEM_LIMIT_MB = 56

f32 = jnp.float32
bf16 = jnp.bfloat16


def _cparams(*sem):
    return pltpu.CompilerParams(dimension_semantics=sem, vmem_limit_bytes=VMEM_LIMIT_MB * 2**20)


def _tile(n, pref, mult=8):
    if n <= pref:
        return n
    for t in range(pref - pref % mult, 0, -mult):
        if n % t == 0:
            return t
    return n


def _dot(a, b):
    return jnp.dot(a, b, preferred_element_type=f32)


def _dot_nt(a, b):
    return lax.dot_general(a, b, (((1,), (1,)), ((), ())), preferred_element_type=f32)


def _dot_tn(a, b):
    return lax.dot_general(a, b, (((0,), (0,)), ((), ())), preferred_element_type=f32)


def _rms_unit(x):
    return x * lax.rsqrt(jnp.mean(x * x, axis=-1, keepdims=True) + RMS_EPS)


def _norm_cast_kernel(x_ref, g_ref, o_ref):
    o_ref[...] = (_rms_unit(x_ref[...]) * g_ref[...]).astype(bf16)


def _norm_cast(x, g):
    M, D = x.shape
    tm = _tile(M, 512)
    return pl.pallas_call(
        _norm_cast_kernel,
        grid=(M // tm,),
        in_specs=[pl.BlockSpec((tm, D), lambda i: (i, 0)), pl.BlockSpec((1, D), lambda i: (0, 0))],
        out_specs=pl.BlockSpec((tm, D), lambda i: (i, 0)),
        out_shape=jax.ShapeDtypeStruct((M, D), bf16),
        compiler_params=_cparams("parallel"),
        name="norm_cast")(x, g)


def _inproj_kernel(h_ref, w_ref, o_ref):
    o_ref[...] = _dot(h_ref[...], w_ref[0])


def _inproj(h, w, li, segs):
    M, D = h.shape
    widths = [b - a for a, b in segs]
    NZ = sum(widths)
    tm, tn = _tile(M, 2048), _tile(math.gcd(*widths), 1024, LANES)
    assert all(a % LANES == 0 for a, _ in segs)

    def src_col(j):
        col, out_start, shift = j * tn, 0, 0
        for (a, _), wd in zip(segs, widths):
            col = col + jnp.where(j * tn >= out_start, (a - out_start) - shift, 0)
            out_start, shift = out_start + wd, a - out_start
        return pl.multiple_of(col, LANES)

    return pl.pallas_call(
        _inproj_kernel,
        grid=(M // tm, NZ // tn),
        in_specs=[pl.BlockSpec((tm, D), lambda i, j: (i, 0)),
                  pl.BlockSpec((pl.Element(1), pl.Element(D), pl.Element(tn)),
                               lambda i, j: (li, 0, src_col(j)))],
        out_specs=pl.BlockSpec((tm, tn), lambda i, j: (i, j)),
        out_shape=jax.ShapeDtypeStruct((M, NZ), f32),
        compiler_params=_cparams("parallel", "parallel"),
        name="inproj")(h, w)


def _mix_core(oa, ci, wa_ref, wc_ref, ga_ref, gc_ref, wo_ref, x_ref, g_ref, gn_ref, o_ref, h_ref):
    ya = _dot(oa, wa_ref[...])
    yc = _dot(ci, wc_ref[...])
    merged = jax.nn.sigmoid(ga_ref[...]) * ya + jax.nn.sigmoid(gc_ref[...]) * yc
    y = _dot(merged.astype(bf16), wo_ref[...])
    x = x_ref[...] + _rms_unit(y) * g_ref[...]
    o_ref[...] = x
    h_ref[...] = (_rms_unit(x) * gn_ref[...]).astype(bf16)


def _mixout_kernel(oa_ref, ci_ref, *rest):
    _mix_core(oa_ref[...], ci_ref[...], *rest)


def _mixout_conv_kernel(oa_ref, cb_ref, cc_ref, ch_ref, ccp_ref, chp_ref, cw_ref, *rest, tiles_per_seq):
    *core, st_ref = rest
    tm = cb_ref.shape[0]
    first = pl.program_id(0) % tiles_per_seq == 0
    u = cc_ref[...] * ch_ref[...]
    up = ccp_ref[...] * chp_ref[...]
    p0 = jnp.where(first, 0.0, up[6:7])
    p1 = jnp.where(first, 0.0, up[7:8])
    row = lax.broadcasted_iota(jnp.int32, u.shape, 0)
    u1 = jnp.where(row == 0, p1, pltpu.roll(u, 1, 0))
    u2 = jnp.where(row == 0, p0, jnp.where(row == 1, p1, pltpu.roll(u, 2, 0)))
    conv = cw_ref[0:1] * u2 + cw_ref[1:2] * u1 + cw_ref[2:3] * u
    st_ref[0] = u[tm - (CONV_W - 1):tm]
    _mix_core(oa_ref[...], (cb_ref[...] * conv).astype(bf16), *core)


def _mixout(oa, ci, wa, wc, z, gate_off, wo, x, g, g_next, li, conv=None):
    M, D = x.shape
    DC = oa.shape[1]
    g0 = gate_off // D
    tm = _tile(M if conv is None else conv[3], 256)
    resident = lambda shape: pl.BlockSpec((None,) + shape, lambda i: (li, 0, 0), pipeline_mode=pl.Buffered(1))
    row = lambda w: pl.BlockSpec((tm, w), lambda i: (i, 0))
    vec = pl.BlockSpec((1, D), lambda i: (0, 0))
    core_specs = [resident((DC, D)), resident((DC, D)),
                  pl.BlockSpec((tm, D), lambda i: (i, g0)), pl.BlockSpec((tm, D), lambda i: (i, g0 + 1)),
                  resident((D, D)), row(D), vec, vec]
    core_args = (wa, wc, z, z, wo, x, g, g_next)
    out_specs = [row(D), row(D)]
    out_shape = [jax.ShapeDtypeStruct((M, D), f32), jax.ShapeDtypeStruct((M, D), bf16)]
    if conv is None:
        return pl.pallas_call(
            _mixout_kernel, grid=(M // tm,),
            in_specs=[row(DC), row(DC)] + core_specs, out_specs=out_specs, out_shape=out_shape,
            compiler_params=_cparams("parallel"), name="mixout")(oa, ci, *core_args)
    conv_off, conv_w, B, T = conv
    c0 = conv_off // DC
    prev = lambda i: (jnp.maximum(i * (tm // 8) - 1, 0))
    zcol = lambda c: pl.BlockSpec((tm, DC), lambda i: (i, c0 + c))
    zprev = lambda c: pl.BlockSpec((8, DC), lambda i: (prev(i), c0 + c))
    return pl.pallas_call(
        functools.partial(_mixout_conv_kernel, tiles_per_seq=T // tm), grid=(M // tm,),
        in_specs=[row(DC), zcol(0), zcol(1), zcol(2), zprev(1), zprev(2),
                  pl.BlockSpec((CONV_W, DC), lambda i: (0, 0))] + core_specs,
        out_specs=out_specs + [pl.BlockSpec((1, CONV_W - 1, DC), lambda i: (i // (T // tm), 0, 0))],
        out_shape=out_shape + [jax.ShapeDtypeStruct((B, CONV_W - 1, DC), f32)],
        compiler_params=_cparams("arbitrary"), name="mixout_conv")(oa, z, z, z, z, z, conv_w, *core_args)


def _ffn_kernel(h_ref, wg_ref, wu_ref, wd_ref, g2_ref, o_ref):
    j = pl.program_id(1)

    @pl.when(j == 0)
    def _():
        o_ref[...] = jnp.zeros_like(o_ref)

    h = h_ref[...]
    a = _dot(h, wg_ref[...])
    u = _dot(h, wu_ref[...])
    act = (a * jax.nn.sigmoid(a) * u).astype(bf16)
    o_ref[...] += _dot(act, wd_ref[...])

    @pl.when(j == pl.num_programs(1) - 1)
    def _():
        o_ref[...] = _rms_unit(o_ref[...]) * g2_ref[...]


def _ffn(h, wg, wu, wd, g2, li):
    M, D = h.shape
    F = wg.shape[2]
    tm, tf = _tile(M, 1024), _tile(F, 512, LANES)
    return pl.pallas_call(
        _ffn_kernel,
        grid=(M // tm, F // tf),
        in_specs=[pl.BlockSpec((tm, D), lambda i, j: (i, 0)),
                  pl.BlockSpec((None, D, tf), lambda i, j: (li, 0, j)),
                  pl.BlockSpec((None, D, tf), lambda i, j: (li, 0, j)),
                  pl.BlockSpec((None, tf, D), lambda i, j: (li, j, 0)),
                  pl.BlockSpec((1, D), lambda i, j: (0, 0))],
        out_specs=pl.BlockSpec((tm, D), lambda i, j: (i, 0)),
        out_shape=jax.ShapeDtypeStruct((M, D), f32),
        compiler_params=_cparams("parallel", "arbitrary"),
        name="ffn")(h, wg, wu, wd, g2)


def _ple_kernel(x_ref, f_ref, p_ref, wi_ref, wg_ref, g_ref, *rest):
    x = x_ref[...] + f_ref[...]
    e = _dot(p_ref[...].astype(bf16), wi_ref[...]) * jax.nn.sigmoid(
        _dot(_rms_unit(x).astype(bf16), wg_ref[...]))
    y = x + _rms_unit(e) * g_ref[...]
    if len(rest) == 1:
        o_ref, = rest
    else:
        gn_ref, o_ref, h_ref = rest
        h_ref[...] = (_rms_unit(y) * gn_ref[...]).astype(bf16)
    o_ref[...] = y


def _ple(x, f, p, wi, wg, g, li, g_next=None):
    M, D = x.shape
    P = p.shape[2]
    tm = _tile(M, 512)
    row = pl.BlockSpec((tm, D), lambda i: (i, 0))
    vec = pl.BlockSpec((1, D), lambda i: (0, 0))
    more = g_next is not None
    return pl.pallas_call(
        _ple_kernel,
        grid=(M // tm,),
        in_specs=[row, row,
                  pl.BlockSpec((None, tm, P), lambda i: (li, i, 0)),
                  pl.BlockSpec((None, P, D), lambda i: (li, 0, 0)),
                  pl.BlockSpec((None, D, D), lambda i: (li, 0, 0)),
                  vec] + [vec] * more,
        out_specs=[row, row] if more else row,
        out_shape=([jax.ShapeDtypeStruct((M, D), f32), jax.ShapeDtypeStruct((M, D), bf16)] if more
                   else jax.ShapeDtypeStruct((M, D), f32)),
        compiler_params=_cparams("parallel"),
        name="ple")(x, f, p, wi, wg, g, *([g_next] * more))


def _conv_sample_kernel(cb_ref, cc_ref, ch_ref, b0_ref, b1_ref, w_ref, o_ref, u_ref):
    u = cc_ref[...] * ch_ref[...]
    conv = w_ref[0:1] * b0_ref[...] + w_ref[1:2] * b1_ref[...] + w_ref[2:3] * u
    o_ref[...] = (cb_ref[...] * conv).astype(bf16)
    u_ref[...] = u


def _conv_sample(z, conv_off, conv_w, buf0, buf1):
    M = z.shape[0]
    DC = conv_w.shape[1]
    c0 = conv_off // DC
    col = lambda c: pl.BlockSpec((M, DC), lambda i: (0, c))
    return pl.pallas_call(
        _conv_sample_kernel,
        grid=(1,),
        in_specs=[col(c0), col(c0 + 1), col(c0 + 2), col(0), col(0),
                  pl.BlockSpec((CONV_W, DC), lambda i: (0, 0))],
        out_specs=[col(0), col(0)],
        out_shape=[jax.ShapeDtypeStruct((M, DC), bf16), jax.ShapeDtypeStruct((M, DC), f32)],
        compiler_params=_cparams("arbitrary"),
        name="conv_sample")(z, z, z, buf0, buf1, conv_w)


def _head_ones():
    r = lax.broadcasted_iota(jnp.int32, (2 * LANES, 2 * LANES), 0) >> 6
    c = lax.broadcasted_iota(jnp.int32, (2 * LANES, 2 * LANES), 1) >> 6
    return (r == c).astype(bf16)


def _head_sum(x, ones):
    hi = x.astype(bf16)
    lo = (x - hi.astype(f32)).astype(bf16)
    s = _dot(jnp.concatenate([hi, lo], axis=1), ones)
    return s[:, :LANES] + s[:, LANES:]


def _wkv_prep(rs, ks, vs, ls, wd0, a0, k_k, k_a, r_k, wda, wgu, ones):
    lane = lax.broadcasted_iota(jnp.int32, rs.shape, 1)
    l0 = ls[:, :LANES]
    lin = jnp.where(lane < R_DECAY, jnp.tanh(l0), l0).astype(bf16)
    da = _dot(lin, wda)
    lw = -math.exp(-0.5) * jax.nn.sigmoid(wd0 + da[:, :LANES])
    a = jax.nn.sigmoid(a0 + da[:, LANES:])
    g = _dot(jax.nn.sigmoid(ls[:, LANES:]).astype(bf16), wgu)
    kkr = ks * k_k
    kap = kkr / jnp.maximum(jnp.sqrt(_head_sum(kkr * kkr, ones)), 1e-12)
    kt = ks * (1.0 + (a - 1.0) * k_a)
    bonus = _head_sum(rs * kt * r_k, ones) * vs
    return lw, a, kap, kt, g, bonus


def _wkv_post(o, bonus, g, ln_w, ln_b, ones):
    inv = 1.0 / HEAD_DIM
    d = o - _head_sum(o, ones) * inv
    var = _head_sum(d * d, ones) * inv
    on = d * lax.rsqrt(var + GN_EPS) * ln_w + ln_b
    return ((on + bonus) * g).astype(bf16)


def _wkv_prompt_kernel(zr_ref, zk_ref, zv_ref, zl_ref, mur_ref, muk_ref, muv_ref, mul_ref,
                       wd0_ref, a0_ref, kk_ref, ka_ref, rk_ref, lnw_ref, lnb_ref, wda_ref, wgu_ref,
                       o_ref, s_ref, st_ref, pr_ref, pk_ref, pv_ref, pl_ref):
    t = pl.program_id(1)
    nb, tb, _ = zr_ref.shape
    rows = nb * tb

    @pl.when(t == 0)
    def _():
        st_ref[...] = jnp.zeros_like(st_ref)
        pr_ref[...] = jnp.zeros_like(pr_ref)
        pk_ref[...] = jnp.zeros_like(pk_ref)
        pv_ref[...] = jnp.zeros_like(pv_ref)
        pl_ref[...] = jnp.zeros_like(pl_ref)

    def shifted(x_ref, p_ref, mu_ref):
        x = x_ref[...]
        w = x.shape[2]
        rolled = pltpu.roll(x.reshape(rows, w), 1, 0).reshape(nb, tb, w)
        row = lax.broadcasted_iota(jnp.int32, x.shape, 1)
        xp = jnp.where(row == 0, p_ref[...], rolled)
        p_ref[...] = x[:, tb - 1:tb, :]
        return (x + mu_ref[...] * (xp - x)).reshape(rows, w)

    rs = shifted(zr_ref, pr_ref, mur_ref)
    ks = shifted(zk_ref, pk_ref, muk_ref)
    vs = shifted(zv_ref, pv_ref, muv_ref)
    ls = shifted(zl_ref, pl_ref, mul_ref)
    ones = _head_ones()
    lw, a, kap, kt, g, bonus = _wkv_prep(rs, ks, vs, ls, wd0_ref[...], a0_ref[...], kk_ref[...],
                                         ka_ref[...], rk_ref[...], wda_ref[0], wgu_ref[...], ones)
    b = kap * a

    ri = lax.broadcasted_iota(jnp.int32, (tb, tb), 0)
    ci = lax.broadcasted_iota(jnp.int32, (tb, tb), 1)
    tri = (((ri >> 6) == (ci >> 6)) & (ci <= ri)).astype(bf16)
    h1 = lw.astype(bf16)
    h2 = (lw - h1.astype(f32)).astype(bf16)
    hl = jnp.concatenate([h1, h2], axis=1)
    cums = []
    for i in range(nb):
        cs = _dot(tri, hl[i * tb:(i + 1) * tb])
        cums.append(cs[:, :LANES] + cs[:, LANES:])

    rr = lax.broadcasted_iota(jnp.int32, (LANES, LANES), 0)
    cc = lax.broadcasted_iota(jnp.int32, (LANES, LANES), 1)
    same_head = (rr >> 6) == (cc >> 6)
    tt, ss = rr & (CHUNK - 1), cc & (CHUNK - 1)
    strict = same_head & (ss < tt)
    incl = same_head & (ss <= tt)
    eye = rr == cc

    def level_mask(k):
        rb, cb = tt >> k, ss >> k
        return same_head & ((rb >> 1) == (cb >> 1)) & ((rb & 1) == 1) & ((cb & 1) == 0)

    def pair(x):
        return jnp.where(same_head, jnp.concatenate([x, x], axis=0), 0.0)

    ncs = tb // CHUNK
    units = [(i, c) for i in range(nb) for c in range(ncs)]
    n = len(units)
    kh, rh, vb, bd, kd, x, y, e_last = [], [], [], [], [], [], [], []
    for i, c in units:
        sl = slice(i * tb + c * CHUNK, i * tb + (c + 1) * CHUNK)
        cm = cums[i][c * CHUNK:(c + 1) * CHUNK]
        c_last = cums[i][(c + 1) * CHUNK - 1:(c + 1) * CHUNK]
        e_p, e_m = jnp.exp(cm), jnp.exp(-cm)
        e_x, e_d = jnp.exp(cm - lw[sl]), jnp.exp(c_last - cm)
        kh.append(pair(kap[sl] * e_x).astype(bf16))
        rh.append(pair(rs[sl] * e_p))
        vb.append(pair(vs[sl]).astype(bf16))
        bd.append(pair(b[sl] * e_d).astype(bf16))
        kd.append(pair(kt[sl] * e_d).astype(bf16))
        x.append(jnp.concatenate([kh[-1], rh[-1].astype(bf16)], axis=0))
        ybc, ykc = (b[sl] * e_m).astype(bf16), (kt[sl] * e_m).astype(bf16)
        y.append(jnp.concatenate([ybc, ybc, ykc, ykc], axis=0))
        e_last.append(jnp.exp(c_last))

    each = lambda f: [f(u) for u in range(n)]
    att = each(lambda u: _dot_nt(x[u], y[u]))
    low = each(lambda u: jnp.where(strict, att[u][:LANES, :LANES], 0.0))
    lowb = each(lambda u: low[u].astype(bf16))
    akk = each(lambda u: jnp.where(strict, att[u][:LANES, LANES:], 0.0).astype(bf16))
    arb_neg = each(lambda u: jnp.where(incl, -att[u][LANES:, :LANES], 0.0).astype(bf16))
    ark = each(lambda u: jnp.where(incl, att[u][LANES:, LANES:], 0.0).astype(bf16))

    m0 = level_mask(0)
    inv = each(lambda u: jnp.where(eye, 1.0, 0.0) - jnp.where(m0, low[u], 0.0))
    for k in range(1, 6):
        mk = level_mask(k)
        ib = each(lambda u: inv[u].astype(bf16))
        t1 = each(lambda u: _dot(ib[u], jnp.where(mk, lowb[u], 0.0)).astype(bf16))
        inv = each(lambda u: inv[u] - _dot(t1[u], ib[u]))
    w = each(lambda u: inv[u].astype(bf16))

    p = each(lambda u: _dot(akk[u], vb[u]).astype(bf16))
    uk = each(lambda u: _dot(w[u], jnp.concatenate([p[u], kh[u]], axis=1)).astype(bf16))
    zero = jnp.zeros((LANES, LANES), bf16)
    q = each(lambda u: _dot(jnp.concatenate([ark[u], arb_neg[u]], axis=1),
                            jnp.concatenate([jnp.concatenate([vb[u], zero], axis=1), uk[u]], axis=0)))
    o0 = each(lambda u: q[u][:, :LANES])
    rbar = each(lambda u: (rh[u] + q[u][:, LANES:]).astype(bf16))
    g0t = each(lambda u: _dot_tn(jnp.concatenate([vb[u], -uk[u][:, :LANES]], axis=0),
                                 jnp.concatenate([kd[u], bd[u]], axis=0)))
    phit = each(lambda u: (jnp.where(eye, e_last[u], 0.0) - _dot_tn(uk[u][:, LANES:], bd[u])).astype(bf16))

    s = [st_ref[i] for i in range(nb)]
    o = [None] * n
    for c in range(ncs):
        for i in range(nb):
            u = i * ncs + c
            sb = s[i].astype(bf16)
            ou = _dot_nt(rbar[u], sb) + o0[u]
            s[i] = _dot(sb, phit[u]) + g0t[u]
            o[u] = ou[:CHUNK] + ou[CHUNK:]
    for i in range(nb):
        st_ref[i] = s[i]

    oa = _wkv_post(jnp.concatenate(o, axis=0), bonus, g, lnw_ref[...], lnb_ref[...], ones)
    o_ref[...] = oa.reshape(nb, tb, LANES)

    @pl.when(t == pl.num_programs(1) - 1)
    def _():
        s_ref[:, 0] = st_ref[...]


def _wkv_prompt(z, zl, lp, B, T):
    DR = lp["ln_x_w"].shape[1]
    nhp = DR // LANES
    tb = _tile(T, 4 * CHUNK, CHUNK)
    lora_blk = 3 * DR // (2 * LANES)
    z3 = z.reshape(B, T, -1)
    zcol = lambda off: pl.BlockSpec((B, tb, LANES), lambda hp, t: (0, t, off + hp))
    vec = lambda off: pl.BlockSpec((1, LANES), lambda hp, t: (0, off + hp))
    in_specs = [zcol(0), zcol(nhp), zcol(2 * nhp),
                pl.BlockSpec((B, tb, 2 * LANES), lambda hp, t: (0, t, 0)),
                vec(0), vec(nhp), vec(2 * nhp),
                pl.BlockSpec((1, 2 * LANES), lambda hp, t: (0, lora_blk)),
                vec(0), vec(0), vec(0), vec(0), vec(0), vec(0), vec(0),
                pl.BlockSpec((1, LANES, 2 * LANES), lambda hp, t: (hp, 0, 0)),
                pl.BlockSpec((R_GATE, LANES), lambda hp, t: (0, hp))]
    oa, s_pair = pl.pallas_call(
        _wkv_prompt_kernel,
        grid=(nhp, T // tb),
        in_specs=in_specs,
        out_specs=[pl.BlockSpec((B, tb, LANES), lambda hp, t: (0, t, hp)),
                   pl.BlockSpec((B, 1, LANES, LANES), lambda hp, t: (0, hp, 0, 0))],
        out_shape=[jax.ShapeDtypeStruct((B, T, DR), bf16),
                   jax.ShapeDtypeStruct((B, nhp, LANES, LANES), f32)],
        scratch_shapes=[pltpu.VMEM((B, LANES, LANES), f32),
                        pltpu.VMEM((B, 1, LANES), f32), pltpu.VMEM((B, 1, LANES), f32),
                        pltpu.VMEM((B, 1, LANES), f32), pltpu.VMEM((B, 1, 2 * LANES), f32)],
        compiler_params=_cparams("parallel", "arbitrary"),
        name="wkv_prompt")(z3, z3, z3, zl.reshape(B, T, -1), lp["mu"], lp["mu"], lp["mu"], lp["mu"],
                           lp["w_decay0"], lp["a0"], lp["k_k"], lp["k_a"], lp["r_k"],
                           lp["ln_x_w"], lp["ln_x_b"], lp["wda"], lp["wgu"])
    return oa.reshape(B * T, DR), s_pair


def _wkv_sample_prep_kernel(zr_ref, zk_ref, zv_ref, zl_ref, sr_ref, sk_ref, sv_ref, sl_ref,
                            mur_ref, muk_ref, muv_ref, mul_ref,
                            wd0_ref, a0_ref, kk_ref, ka_ref, rk_ref, wda_ref, wgu_ref,
                            r_out, w_out, kt_out, v_out, kap_out, b_out, g_out, bonus_out):
    def shifted(x_ref, p_ref, mu_ref):
        x = x_ref[...]
        return x + mu_ref[...] * (p_ref[...] - x)

    rs = shifted(zr_ref, sr_ref, mur_ref)
    ks = shifted(zk_ref, sk_ref, muk_ref)
    vs = shifted(zv_ref, sv_ref, muv_ref)
    ls = shifted(zl_ref, sl_ref, mul_ref)
    ones = _head_ones()
    lw, a, kap, kt, g, bonus = _wkv_prep(rs, ks, vs, ls, wd0_ref[...], a0_ref[...], kk_ref[...],
                                         ka_ref[...], rk_ref[...], wda_ref[0], wgu_ref[...], ones)
    r_out[...] = rs.T
    w_out[...] = jnp.exp(lw).T
    kt_out[...] = kt.T
    v_out[...] = vs.T
    kap_out[...] = kap.T
    b_out[...] = (kap * a).T
    g_out[...] = g
    bonus_out[...] = bonus


def _wkv_sample_prep(z, zl, shift_prev, lp):
    M = z.shape[0]
    DR = lp["ln_x_w"].shape[1]
    nhp = DR // LANES
    lora_blk = 3 * DR // (2 * LANES)
    col = lambda off: pl.BlockSpec((M, LANES), lambda hp: (0, off + hp))
    vec = lambda off: pl.BlockSpec((1, LANES), lambda hp: (0, off + hp))
    wide = lambda rows: pl.BlockSpec((rows, 2 * LANES), lambda hp: (0, lora_blk))
    in_specs = [col(0), col(nhp), col(2 * nhp), pl.BlockSpec((M, 2 * LANES), lambda hp: (0, 0)),
                col(0), col(nhp), col(2 * nhp), wide(M),
                vec(0), vec(nhp), vec(2 * nhp), wide(1),
                vec(0), vec(0), vec(0), vec(0), vec(0),
                pl.BlockSpec((1, LANES, 2 * LANES), lambda hp: (hp, 0, 0)),
                pl.BlockSpec((R_GATE, LANES), lambda hp: (0, hp))]
    return pl.pallas_call(
        _wkv_sample_prep_kernel,
        grid=(nhp,),
        in_specs=in_specs,
        out_specs=[pl.BlockSpec((LANES, M), lambda hp: (hp, 0))] * 6 + [col(0)] * 2,
        out_shape=[jax.ShapeDtypeStruct((DR, M), f32)] * 6 + [jax.ShapeDtypeStruct((M, DR), f32)] * 2,
        compiler_params=_cparams("parallel"),
        name="wkv_sample_prep")(z, z, z, zl, shift_prev, shift_prev, shift_prev, shift_prev,
                                lp["mu"], lp["mu"], lp["mu"], lp["mu"],
                                lp["w_decay0"], lp["a0"], lp["k_k"], lp["k_a"], lp["r_k"],
                                lp["wda"], lp["wgu"])


def _wkv_step_kernel(s_ref, w_ref, kap_ref, b_ref, kt_ref, r_ref, v_ref, so_ref, o_ref):
    for h in range(s_ref.shape[0]):
        s = s_ref[h]
        sa = jnp.sum(s * kap_ref[h][None], axis=1, keepdims=True)
        s_new = s * w_ref[h][None] - sa * b_ref[h][None] + v_ref[h] * kt_ref[h][None]
        so_ref[h] = s_new
        o_ref[h] = jnp.sum(s_new * r_ref[h][None], axis=1, keepdims=True)


def _wkv_step(s, w, kap, b, kt, r, v, li):
    _, H, _, _, nb = s.shape
    ht = _tile(H, 2, 1)
    mat = pl.BlockSpec((ht, HEAD_DIM, HEAD_DIM, nb), lambda i: (i, 0, 0, 0))
    vec = pl.BlockSpec((ht, HEAD_DIM, nb), lambda i: (i, 0, 0))
    col = pl.BlockSpec((ht, HEAD_DIM, 1, nb), lambda i: (i, 0, 0, 0))
    return pl.pallas_call(
        _wkv_step_kernel,
        grid=(H // ht,),
        in_specs=[pl.BlockSpec((None, ht, HEAD_DIM, HEAD_DIM, nb), lambda i: (li, i, 0, 0, 0))] + [vec] * 5 + [col],
        out_specs=[mat, col],
        out_shape=[jax.ShapeDtypeStruct(s.shape[1:], f32), jax.ShapeDtypeStruct((H, HEAD_DIM, 1, nb), f32)],
        compiler_params=_cparams("parallel"),
        name="wkv_step")(s, w, kap, b, kt, r, v)


def _wkv_sample_post_kernel(o_ref, bonus_ref, g_ref, lnw_ref, lnb_ref, oa_ref):
    oa_ref[...] = _wkv_post(o_ref[...].T, bonus_ref[...], g_ref[...], lnw_ref[...], lnb_ref[...], _head_ones())


def _wkv_sample_post(o, bonus, g, lp):
    DR, M = o.shape
    col = pl.BlockSpec((M, LANES), lambda hp: (0, hp))
    vec = pl.BlockSpec((1, LANES), lambda hp: (0, hp))
    return pl.pallas_call(
        _wkv_sample_post_kernel,
        grid=(DR // LANES,),
        in_specs=[pl.BlockSpec((LANES, M), lambda hp: (hp, 0)), col, col, vec, vec],
        out_specs=col,
        out_shape=jax.ShapeDtypeStruct((M, DR), bf16),
        compiler_params=_cparams("parallel"),
        name="wkv_sample_post")(o, bonus, g, lp["ln_x_w"], lp["ln_x_b"])


def _layer_params(i, DR, g_mix_pre, g_mix_post, g_ffn_pre, g_ffn_post, g_ple_post,
                  mu_shift, w_decay0, w_decay_up, a0, w_a_up, w_gate_up, k_k, k_a, r_k,
                  ln_x_w, ln_x_b, conv_w):
    nhp = DR // LANES
    row = lambda v: v[i].reshape(1, -1)
    wdu = w_decay_up[i].reshape(R_DECAY, nhp, LANES).transpose(1, 0, 2)
    wau = w_a_up[i].reshape(R_AAA, nhp, LANES).transpose(1, 0, 2)
    wda = jnp.concatenate([jnp.concatenate([wdu, jnp.zeros_like(wdu)], axis=2),
                           jnp.concatenate([jnp.zeros_like(wau), wau], axis=2)], axis=1).astype(bf16)
    return dict(
        g_mix_pre=row(g_mix_pre), g_mix_post=row(g_mix_post), g_ffn_pre=row(g_ffn_pre),
        g_ffn_post=row(g_ffn_post), g_ple_post=row(g_ple_post),
        mu=row(mu_shift), w_decay0=row(w_decay0), a0=row(a0), k_k=row(k_k), k_a=row(k_a),
        r_k=row(r_k), ln_x_w=row(ln_x_w), ln_x_b=row(ln_x_b), wda=wda, wgu=w_gate_up[i].astype(bf16),
        conv_w=conv_w[i])


def _mix_tail(x, p, z, gate_off, oa, ci, lp, wts, li, g_next, conv=None):
    x, h, *st = _mixout(oa, ci, wts["w_branch_a"], wts["w_branch_c"], z, gate_off, wts["w_out"], x,
                        lp["g_mix_post"], lp["g_ffn_pre"], li, conv)
    f = _ffn(h, wts["w_ffn_gate"], wts["w_ffn_up"], wts["w_ffn_down"], lp["g_ffn_post"], li)
    out = _ple(x, f, p, wts["w_ple_in"], wts["w_ple_gate"], lp["g_ple_post"], li, g_next)
    y, h_next = out if g_next is not None else (out, None)
    return y, h_next, (st[0] if st else None)


def kernel(x_prompt, x_sample, state_wkv, state_shift, state_conv, p_prompt, p_sample, g_mix_pre, g_mix_post, g_ffn_pre, g_ffn_post, g_ple_post, w_in, mu_shift, w_decay0, w_decay_up, a0, w_a_up, w_gate_up, k_k, k_a, r_k, ln_x_w, ln_x_b, conv_w, w_branch_a, w_branch_c, w_out, w_ffn_gate, w_ffn_up, w_ffn_down, w_ple_in, w_ple_gate):
    B, T, D = x_prompt.shape
    DB, DT, _ = x_sample.shape
    depth = w_in.shape[0]
    DR = w_decay0.shape[1]
    DC = conv_w.shape[2]
    H = DR // HEAD_DIM
    d_shift = mu_shift.shape[1]
    assert DT == 1 and T % CHUNK == 0 and T >= CONV_W - 1
    assert DR % (2 * LANES) == 0 and DC == DR and d_shift == 3 * DR + R_DECAY + R_AAA + R_GATE

    wts = dict(
        w_in=w_in.astype(bf16), w_branch_a=w_branch_a.astype(bf16), w_branch_c=w_branch_c.astype(bf16), w_out=w_out.astype(bf16),
        w_ffn_gate=w_ffn_gate.astype(bf16), w_ffn_up=w_ffn_up.astype(bf16),
        w_ffn_down=w_ffn_down.astype(bf16), w_ple_in=w_ple_in.astype(bf16),
        w_ple_gate=w_ple_gate.astype(bf16))
    pp = p_prompt.reshape(depth, B * T, -1)
    ps = p_sample.reshape(depth, DB, -1)
    s_t = jnp.transpose(state_wkv, (0, 2, 3, 4, 1))

    conv_off, gate_off = 3 * DR, 3 * DR + 3 * DC

    def inproj(h, i):
        w = wts["w_in"]
        return (_inproj(h, w, i, [(0, 3 * DR), (d_shift, w_in.shape[2])]),
                _inproj(h, w, i, [(3 * DR, d_shift)]))

    xp = x_prompt.reshape(B * T, D)
    xs = x_sample.reshape(DB, D)
    hp = _norm_cast(xp, g_mix_pre[0].reshape(1, D))
    hs = _norm_cast(xs, g_mix_pre[0].reshape(1, D))
    outs = [[] for _ in range(6)]
    for i in range(depth):
        lp = _layer_params(i, DR, g_mix_pre, g_mix_post, g_ffn_pre, g_ffn_post, g_ple_post,
                           mu_shift, w_decay0, w_decay_up, a0, w_a_up, w_gate_up, k_k, k_a,
                           r_k.reshape(depth, DR), ln_x_w, ln_x_b, conv_w)
        g_next = g_mix_pre[i + 1].reshape(1, D) if i + 1 < depth else None

        z, zl = inproj(hp, i)
        oa, s_pair = _wkv_prompt(z, zl, lp, B, T)
        xp, hp, conv_p = _mix_tail(xp, pp, z, gate_off, oa, None, lp, wts, i, g_next,
                                   conv=(conv_off, lp["conv_w"], B, T))
        s_p = jnp.stack([s_pair[:, :, :HEAD_DIM, :HEAD_DIM], s_pair[:, :, HEAD_DIM:, HEAD_DIM:]], axis=2)
        outs[0].append(s_p.reshape(B, H, HEAD_DIM, HEAD_DIM))
        outs[1].append(jnp.concatenate([z.reshape(B, T, -1)[:, T - 1, :3 * DR], zl.reshape(B, T, -1)[:, T - 1]], axis=1))
        outs[2].append(conv_p)

        z, zl = inproj(hs, i)
        r, w, kt, v, kap, b, g, bonus = _wkv_sample_prep(z, zl, state_shift[i], lp)
        heads = lambda a_: a_.reshape(H, HEAD_DIM, DB)
        s_new, o = _wkv_step(s_t, heads(w), heads(kap), heads(b), heads(kt), heads(r),
                             v.reshape(H, HEAD_DIM, 1, DB), i)
        oa = _wkv_sample_post(o.reshape(DR, DB), bonus, g, lp)
        ci, u = _conv_sample(z, conv_off, lp["conv_w"], state_conv[i][:, 0], state_conv[i][:, 1])
        xs, hs, _ = _mix_tail(xs, ps, z, gate_off, oa, ci, lp, wts, i, g_next)
        outs[3].append(s_new)
        outs[4].append(jnp.concatenate([z[:, :3 * DR], zl], axis=1))
        outs[5].append(jnp.stack([state_conv[i][:, 1], u], axis=1))

    outs = [jnp.stack(o) for o in outs]
    outs[3] = jnp.transpose(outs[3], (0, 4, 1, 2, 3))
    return (xp.reshape(B, T, D), xs.reshape(DB, 1, D)) + tuple(outs)
```

```python
import functools
import math

import jax
import jax.numpy as jnp
from jax import lax
from jax.experimental import pallas as pl
from jax.experimental.pallas import tpu as pltpu

HEAD_DIM = 64
LANES = 128
CHUNK = 64
R_DECAY = 64
R_AAA = 64
R_GATE = 128
CONV_W = 3
RMS_EPS = 1e-6
GN_EPS = 64e-5
VMEM_LIMIT_MB = 56

f32 = jnp.float32
bf16 = jnp.bfloat16


def _cparams(*sem):
    return pltpu.CompilerParams(dimension_semantics=sem, vmem_limit_bytes=VMEM_LIMIT_MB * 2**20)


def _tile(n, pref, mult=8):
    if n <= pref:
        return n
    for t in range(pref - pref % mult, 0, -mult):
        if n % t == 0:
            return t
    return n


def _dot(a, b):
    return jnp.dot(a, b, preferred_element_type=f32)


def _dot_nt(a, b):
    return lax.dot_general(a, b, (((1,), (1,)), ((), ())), preferred_element_type=f32)


def _dot_tn(a, b):
    return lax.dot_general(a, b, (((0,), (0,)), ((), ())), preferred_element_type=f32)


def _rms_unit(x):
    return x * lax.rsqrt(jnp.mean(x * x, axis=-1, keepdims=True) + RMS_EPS)


def _norm_cast_kernel(x_ref, g_ref, o_ref):
    o_ref[...] = (_rms_unit(x_ref[...]) * g_ref[...]).astype(bf16)


def _norm_cast(x, g):
    M, D = x.shape
    tm = _tile(M, 512)
    return pl.pallas_call(
        _norm_cast_kernel,
        grid=(M // tm,),
        in_specs=[pl.BlockSpec((tm, D), lambda i: (i, 0)), pl.BlockSpec((1, D), lambda i: (0, 0))],
        out_specs=pl.BlockSpec((tm, D), lambda i: (i, 0)),
        out_shape=jax.ShapeDtypeStruct((M, D), bf16),
        compiler_params=_cparams("parallel"),
        name="norm_cast")(x, g)


def _inproj_kernel(h_ref, w_ref, o_ref):
    o_ref[...] = _dot(h_ref[...], w_ref[0])


def _inproj(h, w, li, segs):
    M, D = h.shape
    widths = [b - a for a, b in segs]
    NZ = sum(widths)
    tm, tn = _tile(M, 2048), _tile(math.gcd(*widths), 1024, LANES)
    assert all(a % LANES == 0 for a, _ in segs)

    def src_col(j):
        col, out_start, shift = j * tn, 0, 0
        for (a, _), wd in zip(segs, widths):
            col = col + jnp.where(j * tn >= out_start, (a - out_start) - shift, 0)
            out_start, shift = out_start + wd, a - out_start
        return pl.multiple_of(col, LANES)

    return pl.pallas_call(
        _inproj_kernel,
        grid=(M // tm, NZ // tn),
        in_specs=[pl.BlockSpec((tm, D), lambda i, j: (i, 0)),
                  pl.BlockSpec((pl.Element(1), pl.Element(D), pl.Element(tn)),
                               lambda i, j: (li, 0, src_col(j)))],
        out_specs=pl.BlockSpec((tm, tn), lambda i, j: (i, j)),
        out_shape=jax.ShapeDtypeStruct((M, NZ), f32),
        compiler_params=_cparams("parallel", "parallel"),
        name="inproj")(h, w)


def _mix_core(oa, ci, wa_ref, wc_ref, ga_ref, gc_ref, wo_ref, x_ref, g_ref, gn_ref, o_ref, h_ref):
    ya = _dot(oa, wa_ref[...])
    yc = _dot(ci, wc_ref[...])
    merged = jax.nn.sigmoid(ga_ref[...]) * ya + jax.nn.sigmoid(gc_ref[...]) * yc
    y = _dot(merged.astype(bf16), wo_ref[...])
    x = x_ref[...] + _rms_unit(y) * g_ref[...]
    o_ref[...] = x
    h_ref[...] = (_rms_unit(x) * gn_ref[...]).astype(bf16)


def _mixout_kernel(oa_ref, ci_ref, *rest):
    _mix_core(oa_ref[...], ci_ref[...], *rest)


def _mixout_conv_kernel(oa_ref, cb_ref, cc_ref, ch_ref, ccp_ref, chp_ref, cw_ref, *rest, tiles_per_seq):
    *core, st_ref = rest
    tm = cb_ref.shape[0]
    first = pl.program_id(0) % tiles_per_seq == 0
    u = cc_ref[...] * ch_ref[...]
    up = ccp_ref[...] * chp_ref[...]
    p0 = jnp.where(first, 0.0, up[6:7])
    p1 = jnp.where(first, 0.0, up[7:8])
    row = lax.broadcasted_iota(jnp.int32, u.shape, 0)
    u1 = jnp.where(row == 0, p1, pltpu.roll(u, 1, 0))
    u2 = jnp.where(row == 0, p0, jnp.where(row == 1, p1, pltpu.roll(u, 2, 0)))
    conv = cw_ref[0:1] * u2 + cw_ref[1:2] * u1 + cw_ref[2:3] * u
    st_ref[0] = u[tm - (CONV_W - 1):tm]
    _mix_core(oa_ref[...], (cb_ref[...] * conv).astype(bf16), *core)


def _mixout(oa, ci, wa, wc, z, gate_off, wo, x, g, g_next, li, conv=None):
    M, D = x.shape
    DC = oa.shape[1]
    g0 = gate_off // D
    tm = _tile(M if conv is None else conv[3], 256)
    resident = lambda shape: pl.BlockSpec((None,) + shape, lambda i: (li, 0, 0), pipeline_mode=pl.Buffered(1))
    row = lambda w: pl.BlockSpec((tm, w), lambda i: (i, 0))
    vec = pl.BlockSpec((1, D), lambda i: (0, 0))
    core_specs = [resident((DC, D)), resident((DC, D)),
                  pl.BlockSpec((tm, D), lambda i: (i, g0)), pl.BlockSpec((tm, D), lambda i: (i, g0 + 1)),
                  resident((D, D)), row(D), vec, vec]
    core_args = (wa, wc, z, z, wo, x, g, g_next)
    out_specs = [row(D), row(D)]
    out_shape = [jax.ShapeDtypeStruct((M, D), f32), jax.ShapeDtypeStruct((M, D), bf16)]
    if conv is None:
        return pl.pallas_call(
            _mixout_kernel, grid=(M // tm,),
            in_specs=[row(DC), row(DC)] + core_specs, out_specs=out_specs, out_shape=out_shape,
            compiler_params=_cparams("parallel"), name="mixout")(oa, ci, *core_args)
    conv_off, conv_w, B, T = conv
    c0 = conv_off // DC
    prev = lambda i: (jnp.maximum(i * (tm // 8) - 1, 0))
    zcol = lambda c: pl.BlockSpec((tm, DC), lambda i: (i, c0 + c))
    zprev = lambda c: pl.BlockSpec((8, DC), lambda i: (prev(i), c0 + c))
    return pl.pallas_call(
        functools.partial(_mixout_conv_kernel, tiles_per_seq=T // tm), grid=(M // tm,),
        in_specs=[row(DC), zcol(0), zcol(1), zcol(2), zprev(1), zprev(2),
                  pl.BlockSpec((CONV_W, DC), lambda i: (0, 0))] + core_specs,
        out_specs=out_specs + [pl.BlockSpec((1, CONV_W - 1, DC), lambda i: (i // (T // tm), 0, 0))],
        out_shape=out_shape + [jax.ShapeDtypeStruct((B, CONV_W - 1, DC), f32)],
        compiler_params=_cparams("arbitrary"), name="mixout_conv")(oa, z, z, z, z, z, conv_w, *core_args)


def _ffn_kernel(h_ref, wg_ref, wu_ref, wd_ref, g2_ref, o_ref):
    j = pl.program_id(1)

    @pl.when(j == 0)
    def _():
        o_ref[...] = jnp.zeros_like(o_ref)

    h = h_ref[...]
    a = _dot(h, wg_ref[...])
    u = _dot(h, wu_ref[...])
    act = (a * jax.nn.sigmoid(a) * u).astype(bf16)
    o_ref[...] += _dot(act, wd_ref[...])

    @pl.when(j == pl.num_programs(1) - 1)
    def _():
        o_ref[...] = _rms_unit(o_ref[...]) * g2_ref[...]


def _ffn(h, wg, wu, wd, g2, li):
    M, D = h.shape
    F = wg.shape[2]
    tm, tf = _tile(M, 1024), _tile(F, 512, LANES)
    return pl.pallas_call(
        _ffn_kernel,
        grid=(M // tm, F // tf),
        in_specs=[pl.BlockSpec((tm, D), lambda i, j: (i, 0)),
                  pl.BlockSpec((None, D, tf), lambda i, j: (li, 0, j)),
                  pl.BlockSpec((None, D, tf), lambda i, j: (li, 0, j)),
                  pl.BlockSpec((None, tf, D), lambda i, j: (li, j, 0)),
                  pl.BlockSpec((1, D), lambda i, j: (0, 0))],
        out_specs=pl.BlockSpec((tm, D), lambda i, j: (i, 0)),
        out_shape=jax.ShapeDtypeStruct((M, D), f32),
        compiler_params=_cparams("parallel", "arbitrary"),
        name="ffn")(h, wg, wu, wd, g2)


def _ple_kernel(x_ref, f_ref, p_ref, wi_ref, wg_ref, g_ref, *rest):
    x = x_ref[...] + f_ref[...]
    e = _dot(p_ref[...].astype(bf16), wi_ref[...]) * jax.nn.sigmoid(
        _dot(_rms_unit(x).astype(bf16), wg_ref[...]))
    y = x + _rms_unit(e) * g_ref[...]
    if len(rest) == 1:
        o_ref, = rest
    else:
        gn_ref, o_ref, h_ref = rest
        h_ref[...] = (_rms_unit(y) * gn_ref[...]).astype(bf16)
    o_ref[...] = y


def _ple(x, f, p, wi, wg, g, li, g_next=None):
    M, D = x.shape
    P = p.shape[2]
    tm = _tile(M, 512)
    row = pl.BlockSpec((tm, D), lambda i: (i, 0))
    vec = pl.BlockSpec((1, D), lambda i: (0, 0))
    more = g_next is not None
    return pl.pallas_call(
        _ple_kernel,
        grid=(M // tm,),
        in_specs=[row, row,
                  pl.BlockSpec((None, tm, P), lambda i: (li, i, 0)),
                  pl.BlockSpec((None, P, D), lambda i: (li, 0, 0)),
                  pl.BlockSpec((None, D, D), lambda i: (li, 0, 0)),
                  vec] + [vec] * more,
        out_specs=[row, row] if more else row,
        out_shape=([jax.ShapeDtypeStruct((M, D), f32), jax.ShapeDtypeStruct((M, D), bf16)] if more
                   else jax.ShapeDtypeStruct((M, D), f32)),
        compiler_params=_cparams("parallel"),
        name="ple")(x, f, p, wi, wg, g, *([g_next] * more))


def _conv_sample_kernel(cb_ref, cc_ref, ch_ref, b0_ref, b1_ref, w_ref, o_ref, u_ref):
    u = cc_ref[...] * ch_ref[...]
    conv = w_ref[0:1] * b0_ref[...] + w_ref[1:2] * b1_ref[...] + w_ref[2:3] * u
    o_ref[...] = (cb_ref[...] * conv).astype(bf16)
    u_ref[...] = u


def _conv_sample(z, conv_off, conv_w, buf0, buf1):
    M = z.shape[0]
    DC = conv_w.shape[1]
    c0 = conv_off // DC
    col = lambda c: pl.BlockSpec((M, DC), lambda i: (0, c))
    return pl.pallas_call(
        _conv_sample_kernel,
        grid=(1,),
        in_specs=[col(c0), col(c0 + 1), col(c0 + 2), col(0), col(0),
                  pl.BlockSpec((CONV_W, DC), lambda i: (0, 0))],
        out_specs=[col(0), col(0)],
        out_shape=[jax.ShapeDtypeStruct((M, DC), bf16), jax.ShapeDtypeStruct((M, DC), f32)],
        compiler_params=_cparams("arbitrary"),
        name="conv_sample")(z, z, z, buf0, buf1, conv_w)


def _head_ones():
    r = lax.broadcasted_iota(jnp.int32, (2 * LANES, 2 * LANES), 0) >> 6
    c = lax.broadcasted_iota(jnp.int32, (2 * LANES, 2 * LANES), 1) >> 6
    return (r == c).astype(bf16)


def _head_sum(x, ones):
    hi = x.astype(bf16)
    lo = (x - hi.astype(f32)).astype(bf16)
    s = _dot(jnp.concatenate([hi, lo], axis=1), ones)
    return s[:, :LANES] + s[:, LANES:]


def _wkv_prep(rs, ks, vs, ls, wd0, a0, k_k, k_a, r_k, wda, wgu, ones):
    lane = lax.broadcasted_iota(jnp.int32, rs.shape, 1)
    l0 = ls[:, :LANES]
    lin = jnp.where(lane < R_DECAY, jnp.tanh(l0), l0).astype(bf16)
    da = _dot(lin, wda)
    lw = -math.exp(-0.5) * jax.nn.sigmoid(wd0 + da[:, :LANES])
    a = jax.nn.sigmoid(a0 + da[:, LANES:])
    g = _dot(jax.nn.sigmoid(ls[:, LANES:]).astype(bf16), wgu)
    kkr = ks * k_k
    kap = kkr / jnp.maximum(jnp.sqrt(_head_sum(kkr * kkr, ones)), 1e-12)
    kt = ks * (1.0 + (a - 1.0) * k_a)
    bonus = _head_sum(rs * kt * r_k, ones) * vs
    return lw, a, kap, kt, g, bonus


def _wkv_post(o, bonus, g, ln_w, ln_b, ones):
    inv = 1.0 / HEAD_DIM
    d = o - _head_sum(o, ones) * inv
    var = _head_sum(d * d, ones) * inv
    on = d * lax.rsqrt(var + GN_EPS) * ln_w + ln_b
    return ((on + bonus) * g).astype(bf16)


def _wkv_prompt_kernel(zr_ref, zk_ref, zv_ref, zl_ref, mur_ref, muk_ref, muv_ref, mul_ref,
                       wd0_ref, a0_ref, kk_ref, ka_ref, rk_ref, lnw_ref, lnb_ref, wda_ref, wgu_ref,
                       o_ref, s_ref, st_ref, pr_ref, pk_ref, pv_ref, pl_ref):
    t = pl.program_id(1)
    nb, tb, _ = zr_ref.shape
    rows = nb * tb

    @pl.when(t == 0)
    def _():
        st_ref[...] = jnp.zeros_like(st_ref)
        pr_ref[...] = jnp.zeros_like(pr_ref)
        pk_ref[...] = jnp.zeros_like(pk_ref)
        pv_ref[...] = jnp.zeros_like(pv_ref)
        pl_ref[...] = jnp.zeros_like(pl_ref)

    def shifted(x_ref, p_ref, mu_ref):
        x = x_ref[...]
        w = x.shape[2]
        rolled = pltpu.roll(x.reshape(rows, w), 1, 0).reshape(nb, tb, w)
        row = lax.broadcasted_iota(jnp.int32, x.shape, 1)
        xp = jnp.where(row == 0, p_ref[...], rolled)
        p_ref[...] = x[:, tb - 1:tb, :]
        return (x + mu_ref[...] * (xp - x)).reshape(rows, w)

    rs = shifted(zr_ref, pr_ref, mur_ref)
    ks = shifted(zk_ref, pk_ref, muk_ref)
    vs = shifted(zv_ref, pv_ref, muv_ref)
    ls = shifted(zl_ref, pl_ref, mul_ref)
    ones = _head_ones()
    lw, a, kap, kt, g, bonus = _wkv_prep(rs, ks, vs, ls, wd0_ref[...], a0_ref[...], kk_ref[...],
                                         ka_ref[...], rk_ref[...], wda_ref[0], wgu_ref[...], ones)
    b = kap * a

    ri = lax.broadcasted_iota(jnp.int32, (tb, tb), 0)
    ci = lax.broadcasted_iota(jnp.int32, (tb, tb), 1)
    tri = (((ri >> 6) == (ci >> 6)) & (ci <= ri)).astype(bf16)
    h1 = lw.astype(bf16)
    h2 = (lw - h1.astype(f32)).astype(bf16)
    hl = jnp.concatenate([h1, h2], axis=1)
    cums = []
    for i in range(nb):
        cs = _dot(tri, hl[i * tb:(i + 1) * tb])
        cums.append(cs[:, :LANES] + cs[:, LANES:])

    rr = lax.broadcasted_iota(jnp.int32, (LANES, LANES), 0)
    cc = lax.broadcasted_iota(jnp.int32, (LANES, LANES), 1)
    same_head = (rr >> 6) == (cc >> 6)
    tt, ss = rr & (CHUNK - 1), cc & (CHUNK - 1)
    strict = same_head & (ss < tt)
    incl = same_head & (ss <= tt)
    eye = rr == cc

    def level_mask(k):
        rb, cb = tt >> k, ss >> k
        return same_head & ((rb >> 1) == (cb >> 1)) & ((rb & 1) == 1) & ((cb & 1) == 0)

    def pair(x):
        return jnp.where(same_head, jnp.concatenate([x, x], axis=0), 0.0)

    ncs = tb // CHUNK
    units = [(i, c) for i in range(nb) for c in range(ncs)]
    n = len(units)
    kh, rh, vb, bd, kd, x, y, e_last = [], [], [], [], [], [], [], []
    for i, c in units:
        sl = slice(i * tb + c * CHUNK, i * tb + (c + 1) * CHUNK)
        cm = cums[i][c * CHUNK:(c + 1) * CHUNK]
        c_last = cums[i][(c + 1) * CHUNK - 1:(c + 1) * CHUNK]
        e_p, e_m = jnp.exp(cm), jnp.exp(-cm)
        e_x, e_d = jnp.exp(cm - lw[sl]), jnp.exp(c_last - cm)
        kh.append(pair(kap[sl] * e_x).astype(bf16))
        rh.append(pair(rs[sl] * e_p))
        vb.append(pair(vs[sl]).astype(bf16))
        bd.append(pair(b[sl] * e_d).astype(bf16))
        kd.append(pair(kt[sl] * e_d).astype(bf16))
        x.append(jnp.concatenate([kh[-1], rh[-1].astype(bf16)], axis=0))
        ybc, ykc = (b[sl] * e_m).astype(bf16), (kt[sl] * e_m).astype(bf16)
        y.append(jnp.concatenate([ybc, ybc, ykc, ykc], axis=0))
        e_last.append(jnp.exp(c_last))

    each = lambda f: [f(u) for u in range(n)]
    att = each(lambda u: _dot_nt(x[u], y[u]))
    low = each(lambda u: jnp.where(strict, att[u][:LANES, :LANES], 0.0))
    lowb = each(lambda u: low[u].astype(bf16))
    akk = each(lambda u: jnp.where(strict, att[u][:LANES, LANES:], 0.0).astype(bf16))
    arb_neg = each(lambda u: jnp.where(incl, -att[u][LANES:, :LANES], 0.0).astype(bf16))
    ark = each(lambda u: jnp.where(incl, att[u][LANES:, LANES:], 0.0).astype(bf16))

    m0 = level_mask(0)
    inv = each(lambda u: jnp.where(eye, 1.0, 0.0) - jnp.where(m0, low[u], 0.0))
    for k in range(1, 6):
        mk = level_mask(k)
        ib = each(lambda u: inv[u].astype(bf16))
        t1 = each(lambda u: _dot(ib[u], jnp.where(mk, lowb[u], 0.0)).astype(bf16))
        inv = each(lambda u: inv[u] - _dot(t1[u], ib[u]))
    w = each(lambda u: inv[u].astype(bf16))

    p = each(lambda u: _dot(akk[u], vb[u]).astype(bf16))
    uk = each(lambda u: _dot(w[u], jnp.concatenate([p[u], kh[u]], axis=1)).astype(bf16))
    zero = jnp.zeros((LANES, LANES), bf16)
    q = each(lambda u: _dot(jnp.concatenate([ark[u], arb_neg[u]], axis=1),
                            jnp.concatenate([jnp.concatenate([vb[u], zero], axis=1), uk[u]], axis=0)))
    o0 = each(lambda u: q[u][:, :LANES])
    rbar = each(lambda u: (rh[u] + q[u][:, LANES:]).astype(bf16))
    g0t = each(lambda u: _dot_tn(jnp.concatenate([vb[u], -uk[u][:, :LANES]], axis=0),
                                 jnp.concatenate([kd[u], bd[u]], axis=0)))
    phit = each(lambda u: (jnp.where(eye, e_last[u], 0.0) - _dot_tn(uk[u][:, LANES:], bd[u])).astype(bf16))

    s = [st_ref[i] for i in range(nb)]
    o = [None] * n
    for c in range(ncs):
        for i in range(nb):
            u = i * ncs + c
            sb = s[i].astype(bf16)
            ou = _dot_nt(rbar[u], sb) + o0[u]
            s[i] = _dot(sb, phit[u]) + g0t[u]
            o[u] = ou[:CHUNK] + ou[CHUNK:]
    for i in range(nb):
        st_ref[i] = s[i]

    oa = _wkv_post(jnp.concatenate(o, axis=0), bonus, g, lnw_ref[...], lnb_ref[...], ones)
    o_ref[...] = oa.reshape(nb, tb, LANES)

    @pl.when(t == pl.num_programs(1) - 1)
    def _():
        s_ref[:, 0] = st_ref[...]


def _wkv_prompt(z, zl, lp, B, T):
    DR = lp["ln_x_w"].shape[1]
    nhp = DR // LANES
    tb = _tile(T, 4 * CHUNK, CHUNK)
    lora_blk = 3 * DR // (2 * LANES)
    z3 = z.reshape(B, T, -1)
    zcol = lambda off: pl.BlockSpec((B, tb, LANES), lambda hp, t: (0, t, off + hp))
    vec = lambda off: pl.BlockSpec((1, LANES), lambda hp, t: (0, off + hp))
    in_specs = [zcol(0), zcol(nhp), zcol(2 * nhp),
                pl.BlockSpec((B, tb, 2 * LANES), lambda hp, t: (0, t, 0)),
                vec(0), vec(nhp), vec(2 * nhp),
                pl.BlockSpec((1, 2 * LANES), lambda hp, t: (0, lora_blk)),
                vec(0), vec(0), vec(0), vec(0), vec(0), vec(0), vec(0),
                pl.BlockSpec((1, LANES, 2 * LANES), lambda hp, t: (hp, 0, 0)),
                pl.BlockSpec((R_GATE, LANES), lambda hp, t: (0, hp))]
    oa, s_pair = pl.pallas_call(
        _wkv_prompt_kernel,
        grid=(nhp, T // tb),
        in_specs=in_specs,
        out_specs=[pl.BlockSpec((B, tb, LANES), lambda hp, t: (0, t, hp)),
                   pl.BlockSpec((B, 1, LANES, LANES), lambda hp, t: (0, hp, 0, 0))],
        out_shape=[jax.ShapeDtypeStruct((B, T, DR), bf16),
                   jax.ShapeDtypeStruct((B, nhp, LANES, LANES), f32)],
        scratch_shapes=[pltpu.VMEM((B, LANES, LANES), f32),
                        pltpu.VMEM((B, 1, LANES), f32), pltpu.VMEM((B, 1, LANES), f32),
                        pltpu.VMEM((B, 1, LANES), f32), pltpu.VMEM((B, 1, 2 * LANES), f32)],
        compiler_params=_cparams("parallel", "arbitrary"),
        name="wkv_prompt")(z3, z3, z3, zl.reshape(B, T, -1), lp["mu"], lp["mu"], lp["mu"], lp["mu"],
                           lp["w_decay0"], lp["a0"], lp["k_k"], lp["k_a"], lp["r_k"],
                           lp["ln_x_w"], lp["ln_x_b"], lp["wda"], lp["wgu"])
    return oa.reshape(B * T, DR), s_pair


def _wkv_sample_prep_kernel(zr_ref, zk_ref, zv_ref, zl_ref, sr_ref, sk_ref, sv_ref, sl_ref,
                            mur_ref, muk_ref, muv_ref, mul_ref,
                            wd0_ref, a0_ref, kk_ref, ka_ref, rk_ref, wda_ref, wgu_ref,
                            r_out, w_out, kt_out, v_out, kap_out, b_out, g_out, bonus_out):
    def shifted(x_ref, p_ref, mu_ref):
        x = x_ref[...]
        return x + mu_ref[...] * (p_ref[...] - x)

    rs = shifted(zr_ref, sr_ref, mur_ref)
    ks = shifted(zk_ref, sk_ref, muk_ref)
    vs = shifted(zv_ref, sv_ref, muv_ref)
    ls = shifted(zl_ref, sl_ref, mul_ref)
    ones = _head_ones()
    lw, a, kap, kt, g, bonus = _wkv_prep(rs, ks, vs, ls, wd0_ref[...], a0_ref[...], kk_ref[...],
                                         ka_ref[...], rk_ref[...], wda_ref[0], wgu_ref[...], ones)
    r_out[...] = rs.T
    w_out[...] = jnp.exp(lw).T
    kt_out[...] = kt.T
    v_out[...] = vs.T
    kap_out[...] = kap.T
    b_out[...] = (kap * a).T
    g_out[...] = g
    bonus_out[...] = bonus


def _wkv_sample_prep(z, zl, shift_prev, lp):
    M = z.shape[0]
    DR = lp["ln_x_w"].shape[1]
    nhp = DR // LANES
    lora_blk = 3 * DR // (2 * LANES)
    col = lambda off: pl.BlockSpec((M, LANES), lambda hp: (0, off + hp))
    vec = lambda off: pl.BlockSpec((1, LANES), lambda hp: (0, off + hp))
    wide = lambda rows: pl.BlockSpec((rows, 2 * LANES), lambda hp: (0, lora_blk))
    in_specs = [col(0), col(nhp), col(2 * nhp), pl.BlockSpec((M, 2 * LANES), lambda hp: (0, 0)),
                col(0), col(nhp), col(2 * nhp), wide(M),
                vec(0), vec(nhp), vec(2 * nhp), wide(1),
                vec(0), vec(0), vec(0), vec(0), vec(0),
                pl.BlockSpec((1, LANES, 2 * LANES), lambda hp: (hp, 0, 0)),
                pl.BlockSpec((R_GATE, LANES), lambda hp: (0, hp))]
    return pl.pallas_call(
        _wkv_sample_prep_kernel,
        grid=(nhp,),
        in_specs=in_specs,
        out_specs=[pl.BlockSpec((LANES, M), lambda hp: (hp, 0))] * 6 + [col(0)] * 2,
        out_shape=[jax.ShapeDtypeStruct((DR, M), f32)] * 6 + [jax.ShapeDtypeStruct((M, DR), f32)] * 2,
        compiler_params=_cparams("parallel"),
        name="wkv_sample_prep")(z, z, z, zl, shift_prev, shift_prev, shift_prev, shift_prev,
                                lp["mu"], lp["mu"], lp["mu"], lp["mu"],
                                lp["w_decay0"], lp["a0"], lp["k_k"], lp["k_a"], lp["r_k"],
                                lp["wda"], lp["wgu"])


def _wkv_step_kernel(s_ref, w_ref, kap_ref, b_ref, kt_ref, r_ref, v_ref, carried_ref, so_ref, o_ref):
    del carried_ref
    for h in range(s_ref.shape[0]):
        s = s_ref[h]
        sa = jnp.sum(s * kap_ref[h][None], axis=1, keepdims=True)
        s_new = s * w_ref[h][None] - sa * b_ref[h][None] + v_ref[h] * kt_ref[h][None]
        so_ref[h] = s_new
        o_ref[h] = jnp.sum(s_new * r_ref[h][None], axis=1, keepdims=True)


def _wkv_step(s, w, kap, b, kt, r, v, li, s_new):
    _, H, _, _, nb = s.shape
    ht = _tile(H, 2, 1)
    mat = pl.BlockSpec((None, ht, HEAD_DIM, HEAD_DIM, nb), lambda i: (li, i, 0, 0, 0))
    vec = pl.BlockSpec((ht, HEAD_DIM, nb), lambda i: (i, 0, 0))
    col = pl.BlockSpec((ht, HEAD_DIM, 1, nb), lambda i: (i, 0, 0, 0))
    return pl.pallas_call(
        _wkv_step_kernel,
        grid=(H // ht,),
        in_specs=[mat] + [vec] * 5 + [col, pl.BlockSpec(memory_space=pl.ANY)],
        out_specs=[mat, col],
        out_shape=[jax.ShapeDtypeStruct(s.shape, f32), jax.ShapeDtypeStruct((H, HEAD_DIM, 1, nb), f32)],
        input_output_aliases={7: 0},
        compiler_params=_cparams("parallel"),
        name="wkv_step")(s, w, kap, b, kt, r, v, s_new)


def _wkv_sample_post_kernel(o_ref, bonus_ref, g_ref, lnw_ref, lnb_ref, oa_ref):
    oa_ref[...] = _wkv_post(o_ref[...].T, bonus_ref[...], g_ref[...], lnw_ref[...], lnb_ref[...], _head_ones())


def _wkv_sample_post(o, bonus, g, lp):
    DR, M = o.shape
    col = pl.BlockSpec((M, LANES), lambda hp: (0, hp))
    vec = pl.BlockSpec((1, LANES), lambda hp: (0, hp))
    return pl.pallas_call(
        _wkv_sample_post_kernel,
        grid=(DR // LANES,),
        in_specs=[pl.BlockSpec((LANES, M), lambda hp: (hp, 0)), col, col, vec, vec],
        out_specs=col,
        out_shape=jax.ShapeDtypeStruct((M, DR), bf16),
        compiler_params=_cparams("parallel"),
        name="wkv_sample_post")(o, bonus, g, lp["ln_x_w"], lp["ln_x_b"])


def _layer_params(i, DR, g_mix_post, g_ffn_pre, g_ffn_post, g_ple_post,
                  mu_shift, w_decay0, w_decay_up, a0, w_a_up, w_gate_up, k_k, k_a, r_k,
                  ln_x_w, ln_x_b, conv_w):
    nhp = DR // LANES
    row = lambda v: v[i].reshape(1, -1)
    wdu = w_decay_up[i].reshape(R_DECAY, nhp, LANES).transpose(1, 0, 2)
    wau = w_a_up[i].reshape(R_AAA, nhp, LANES).transpose(1, 0, 2)
    wda = jnp.concatenate([jnp.concatenate([wdu, jnp.zeros_like(wdu)], axis=2),
                           jnp.concatenate([jnp.zeros_like(wau), wau], axis=2)], axis=1).astype(bf16)
    return dict(
        g_mix_post=row(g_mix_post), g_ffn_pre=row(g_ffn_pre),
        g_ffn_post=row(g_ffn_post), g_ple_post=row(g_ple_post),
        mu=row(mu_shift), w_decay0=row(w_decay0), a0=row(a0), k_k=row(k_k), k_a=row(k_a),
        r_k=row(r_k), ln_x_w=row(ln_x_w), ln_x_b=row(ln_x_b), wda=wda, wgu=w_gate_up[i].astype(bf16),
        conv_w=conv_w[i])


def _mix_tail(x, p, z, gate_off, oa, ci, lp, wts, li, g_next, conv=None):
    x, h, *st = _mixout(oa, ci, wts["w_branch_a"], wts["w_branch_c"], z, gate_off, wts["w_out"], x,
                        lp["g_mix_post"], lp["g_ffn_pre"], li, conv)
    f = _ffn(h, wts["w_ffn_gate"], wts["w_ffn_up"], wts["w_ffn_down"], lp["g_ffn_post"], li)
    out = _ple(x, f, p, wts["w_ple_in"], wts["w_ple_gate"], lp["g_ple_post"], li, g_next)
    y, h_next = out if g_next is not None else (out, None)
    return y, h_next, (st[0] if st else None)


def kernel(x_prompt, x_sample, state_wkv, state_shift, state_conv, p_prompt, p_sample, g_mix_pre, g_mix_post, g_ffn_pre, g_ffn_post, g_ple_post, w_in, mu_shift, w_decay0, w_decay_up, a0, w_a_up, w_gate_up, k_k, k_a, r_k, ln_x_w, ln_x_b, conv_w, w_branch_a, w_branch_c, w_out, w_ffn_gate, w_ffn_up, w_ffn_down, w_ple_in, w_ple_gate):
    B, T, D = x_prompt.shape
    DB, DT, _ = x_sample.shape
    depth = w_in.shape[0]
    DR = w_decay0.shape[1]
    DC = conv_w.shape[2]
    H = DR // HEAD_DIM
    d_shift = mu_shift.shape[1]
    assert DT == 1 and T % CHUNK == 0 and T >= CONV_W - 1
    assert DR % (2 * LANES) == 0 and DC == DR and d_shift == 3 * DR + R_DECAY + R_AAA + R_GATE

    wts = dict(
        w_in=w_in.astype(bf16), w_branch_a=w_branch_a.astype(bf16), w_branch_c=w_branch_c.astype(bf16), w_out=w_out.astype(bf16),
        w_ffn_gate=w_ffn_gate.astype(bf16), w_ffn_up=w_ffn_up.astype(bf16),
        w_ffn_down=w_ffn_down.astype(bf16), w_ple_in=w_ple_in.astype(bf16),
        w_ple_gate=w_ple_gate.astype(bf16))
    pp = p_prompt.reshape(depth, B * T, -1)
    ps = p_sample.reshape(depth, DB, -1)
    s_t = jnp.transpose(state_wkv, (0, 2, 3, 4, 1))

    conv_off, gate_off = 3 * DR, 3 * DR + 3 * DC

    def inproj(h, i):
        w = wts["w_in"]
        return (_inproj(h, w, i, [(0, 3 * DR), (d_shift, w_in.shape[2])]),
                _inproj(h, w, i, [(3 * DR, d_shift)]))

    xp = x_prompt.reshape(B * T, D)
    xs = x_sample.reshape(DB, D)
    hp = _norm_cast(xp, g_mix_pre[0].reshape(1, D))
    hs = _norm_cast(xs, g_mix_pre[0].reshape(1, D))
    outs = [[] for _ in range(6)]
    s_new = jnp.zeros(s_t.shape, f32)
    for i in range(depth):
        lp = _layer_params(i, DR, g_mix_post, g_ffn_pre, g_ffn_post, g_ple_post,
                           mu_shift, w_decay0, w_decay_up, a0, w_a_up, w_gate_up, k_k, k_a,
                           r_k.reshape(depth, DR), ln_x_w, ln_x_b, conv_w)
        g_next = g_mix_pre[i + 1].reshape(1, D) if i + 1 < depth else None

        z, zl = inproj(hp, i)
        oa, s_pair = _wkv_prompt(z, zl, lp, B, T)
        xp, hp, conv_p = _mix_tail(xp, pp, z, gate_off, oa, None, lp, wts, i, g_next,
                                   conv=(conv_off, lp["conv_w"], B, T))
        s_p = jnp.stack([s_pair[:, :, :HEAD_DIM, :HEAD_DIM], s_pair[:, :, HEAD_DIM:, HEAD_DIM:]], axis=2)
        outs[0].append(s_p.reshape(B, H, HEAD_DIM, HEAD_DIM))
        outs[1].append(jnp.concatenate([z.reshape(B, T, -1)[:, T - 1, :3 * DR], zl.reshape(B, T, -1)[:, T - 1]], axis=1))
        outs[2].append(conv_p)

        z, zl = inproj(hs, i)
        r, w, kt, v, kap, b, g, bonus = _wkv_sample_prep(z, zl, state_shift[i], lp)
        heads = lambda a_: a_.reshape(H, HEAD_DIM, DB)
        s_new, o = _wkv_step(s_t, heads(w), heads(kap), heads(b), heads(kt), heads(r),
                             v.reshape(H, HEAD_DIM, 1, DB), i, s_new)
        oa = _wkv_sample_post(o.reshape(DR, DB), bonus, g, lp)
        ci, u = _conv_sample(z, conv_off, lp["conv_w"], state_conv[i][:, 0], state_conv[i][:, 1])
        xs, hs, _ = _mix_tail(xs, ps, z, gate_off, oa, ci, lp, wts, i, g_next)
        outs[4].append(jnp.concatenate([z[:, :3 * DR], zl], axis=1))
        outs[5].append(jnp.stack([state_conv[i][:, 1], u], axis=1))

    outs[3] = jnp.transpose(s_new, (0, 4, 1, 2, 3))
    outs = [o if i == 3 else jnp.stack(o) for i, o in enumerate(outs)]
    return (xp.reshape(B, T, D), xs.reshape(DB, 1, D)) + tuple(outs)
```
